```python
import jax
import jax.numpy as jnp
from jax import lax
import numpy as np


D_MODEL = 1024
BATCH = 8
SEQ = 4096
DEPTH = 4

GRID_W = 64
CTX_LEN = 256
N_MIXERS = 2
CONV_WIDTH = 31
GLA_HEADS = 4
GLA_DK = D_MODEL // 2
GLA_DV = D_MODEL
HEAD_K = GLA_DK // GLA_HEADS
HEAD_V = GLA_DV // GLA_HEADS
GATE_RANK = 16
GATE_TAU = 16.0
CHUNK = 64
FFN_DIM = 2816
FFN_CONV = 3
EPS = 1e-6

kernel_name = 'hybrid_conformer_gla_convffn_dit'


def rms_norm(x, g):
    x32 = x.astype(jnp.float32)
    y = x32 * lax.rsqrt(jnp.mean(x32 * x32, axis=-1, keepdims=True) + EPS)
    return (y * g.astype(jnp.float32)).astype(x.dtype)


def layer_norm(x, g, b):
    x32 = x.astype(jnp.float32)
    mu = jnp.mean(x32, axis=-1, keepdims=True)
    var = jnp.mean(jnp.square(x32 - mu), axis=-1, keepdims=True)
    y = (x32 - mu) * lax.rsqrt(var + EPS)
    return (y * g.astype(jnp.float32) + b.astype(jnp.float32)).astype(x.dtype)


def adaln(cond, w, b):
    m = jax.nn.silu(cond) @ w + b
    return jnp.split(m, 6, axis=-1)


def modulate(h, shift, scale):
    return h * (1.0 + scale) + shift


def dwconv1d(x, w, b):
    pad = w.shape[0] // 2
    y = lax.conv_general_dilated(x, w[:, None, :], window_strides=(1,), padding=[(pad, pad)],
                                 dimension_numbers=('NWC', 'WIO', 'NWC'), feature_group_count=x.shape[-1])
    return y + b


def dwconv2d(x, w, b):
    y = lax.conv_general_dilated(x, w[:, :, None, :], window_strides=(1, 1), padding='SAME',
                                 dimension_numbers=('NHWC', 'HWIO', 'NHWC'), feature_group_count=x.shape[-1])
    return y + b


def conformer_conv(h, w1, b1, dw, dwb, ln_g, ln_b, w2, b2):
    u = h @ w1 + b1
    a, gt = jnp.split(u, 2, axis=-1)
    u = dwconv1d(a * jax.nn.sigmoid(gt), dw, dwb)
    u = jax.nn.silu(layer_norm(u, ln_g, ln_b))
    return u @ w2 + b2


def conv_ffn(h, wa, wb, dw, dwb, wo, on_grid):
    a = h @ wa
    if on_grid:
        bsz, n, f = a.shape
        rows = n // GRID_W
        a = dwconv2d(a.reshape(bsz, rows, GRID_W, f), dw, dwb).reshape(bsz, n, f)
    else:
        a = dwconv1d(a, dw[FFN_CONV // 2], dwb)
    return (jax.nn.silu(a) * (h @ wb)) @ wo


def split_heads(t, hd):
    return t.reshape(t.shape[0], t.shape[1], -1, hd)


def gla_log_gate(h, wg1, wg2, bg):
    z = ((h @ wg1) @ wg2 + bg).astype(jnp.float32)
    return split_heads(jax.nn.log_sigmoid(z) / GATE_TAU, HEAD_K)


def gla_chunk_scan(q, k, v, g, s0):
    bsz, n, nh, dk = q.shape
    dv = v.shape[-1]
    nc = n // CHUNK
    mask = jnp.tril(jnp.ones((CHUNK, CHUNK), dtype=bool))[:, :, None]

    def to_chunks(t):
        return t.reshape(bsz, nc, CHUNK, nh, t.shape[-1]).transpose(1, 0, 3, 2, 4)

    def step(s, inp):
        qc, kc, vc, gc = inp
        b = jnp.cumsum(gc, axis=2)
        o_inter = jnp.einsum('bhtd,bhde->bhte', qc * jnp.exp(b), s)
        diff = b[:, :, :, None, :] - b[:, :, None, :, :]
        decay = jnp.exp(jnp.where(mask, diff, -jnp.inf))
        scores = jnp.einsum('bhtd,bhsd,bhtsd->bhts', qc, kc, decay)
        o_intra = jnp.einsum('bhts,bhse->bhte', scores, vc)
        b_end = b[:, :, -1:, :]
        s_new = jnp.exp(b_end[:, :, 0, :])[..., None] * s + jnp.einsum('bhsd,bhse->bhde', kc * jnp.exp(b_end - b), vc)
        return s_new, o_inter + o_intra

    s_fin, o = lax.scan(step, s0, (to_chunks(q), to_chunks(k), to_chunks(v), to_chunks(g)))
    o = o.transpose(1, 0, 3, 2, 4).reshape(bsz, n, nh, dv)
    return o, s_fin


def gla_bidir(q, k, v, g_f, g_b, s_f, s_b):
    o_f, fin_f = gla_chunk_scan(q, k, v, g_f, s_f)
    o_b, fin_b = gla_chunk_scan(q[:, ::-1], k[:, ::-1], v[:, ::-1], g_b[:, ::-1], s_b)
    return o_f + o_b[:, ::-1], fin_f, fin_b


def gla_final_state(k, v, g):
    b = jnp.cumsum(g, axis=1)
    return jnp.einsum('bnhd,bnhe->bhde', k * jnp.exp(b[:, -1:] - b), v)


def gla_mixer(hx, hc, wq, wk, wv, wr, wg1, wg2, bg, norm_g, wo, ctx_out):
    def proj_q(h):
        return split_heads((h @ wq).astype(jnp.float32), HEAD_K) * (HEAD_K ** -0.5)

    def proj_kvg(h):
        k = split_heads((h @ wk).astype(jnp.float32), HEAD_K)
        v = split_heads((h @ wv).astype(jnp.float32), HEAD_V)
        return k, v, gla_log_gate(h, wg1[0], wg2[0], bg[0]), gla_log_gate(h, wg1[1], wg2[1], bg[1])

    def readout(o, h):
        o = rms_norm(o, norm_g).reshape(h.shape[0], h.shape[1], GLA_DV).astype(h.dtype)
        return (o * jax.nn.silu(h @ wr)) @ wo

    kc, vc, gcf, gcb = proj_kvg(hc)
    if ctx_out:
        zeros = jnp.zeros((hc.shape[0], GLA_HEADS, HEAD_K, HEAD_V), jnp.float32)
        oc, s_f, s_b = gla_bidir(proj_q(hc), kc, vc, gcf, gcb, zeros, zeros)
        yc = readout(oc, hc)
    else:
        s_f = gla_final_state(kc, vc, gcf)
        s_b = gla_final_state(kc[:, ::-1], vc[:, ::-1], gcb[:, ::-1])
        yc = None
    kx, vx, gxf, gxb = proj_kvg(hx)
    ox, _, _ = gla_bidir(proj_q(hx), kx, vx, gxf, gxb, s_f, s_b)
    return readout(ox, hx), yc


def setup_inputs(seed: int = 0) -> dict:
    key = jax.random.key(seed)
    ks = iter(jax.random.split(key, 40))
    n_a = (DEPTH + 1) // 2
    n_b = DEPTH // 2
    D = D_MODEL

    def nrm(shape, scale):
        return jax.random.normal(next(ks), shape, jnp.float32) * scale

    def gain(shape):
        return 1.0 + nrm(shape, 0.05)

    return {
        'x': nrm((BATCH, SEQ, D), 1.0),
        'c': nrm((BATCH, D), 1.0),
        'ctx': nrm((BATCH, CTX_LEN, D), 1.0),
        'c_ctx': nrm((D,), 1.0),
        'ada_w': nrm((DEPTH, D, 6 * D), 0.5 * D ** -0.5),
        'ada_b': nrm((DEPTH, 6 * D), 0.02),
        'norm_pre_mix': gain((DEPTH, D)),
        'norm_post_mix': gain((DEPTH, D)),
        'norm_pre_ffn': gain((DEPTH, D)),
        'norm_post_ffn': gain((DEPTH, D)),
        'cf_w1': nrm((n_a, D, 2 * D), D ** -0.5),
        'cf_b1': nrm((n_a, 2 * D), 0.02),
        'cf_dw': nrm((n_a, CONV_WIDTH, D), CONV_WIDTH ** -0.5),
        'cf_dwb': nrm((n_a, D), 0.02),
        'cf_ln_g': gain((n_a, D)),
        'cf_ln_b': nrm((n_a, D), 0.02),
        'cf_w2': nrm((n_a, D, D), D ** -0.5),
        'cf_b2': nrm((n_a, D), 0.02),
        'gla_wq': nrm((n_b, D, GLA_DK), D ** -0.5),
        'gla_wk': nrm((n_b, D, GLA_DK), D ** -0.5),
        'gla_wv': nrm((n_b, D, GLA_DV), D ** -0.5),
        'gla_wr': nrm((n_b, D, GLA_DV), D ** -0.5),
        'gla_wg1': nrm((n_b, 2, D, GATE_RANK), D ** -0.5),
        'gla_wg2': nrm((n_b, 2, GATE_RANK, GLA_DK), GATE_RANK ** -0.5),
        'gla_bg': nrm((n_b, 2, GLA_DK), 0.5),
        'gla_norm_g': gain((n_b, HEAD_V)),
        'gla_wo': nrm((n_b, GLA_DV, D), GLA_DV ** -0.5),
        'ffn_wa': nrm((DEPTH, D, FFN_DIM), D ** -0.5),
        'ffn_wb': nrm((DEPTH, D, FFN_DIM), D ** -0.5),
        'ffn_dw': nrm((DEPTH, FFN_CONV, FFN_CONV, FFN_DIM), 1.0 / FFN_CONV),
        'ffn_dwb': nrm((DEPTH, FFN_DIM), 0.02),
        'ffn_wo': nrm((DEPTH, FFN_DIM, D), FFN_DIM ** -0.5),
    }


def reference(x, c, ctx, c_ctx, ada_w, ada_b, norm_pre_mix, norm_post_mix, norm_pre_ffn, norm_post_ffn,
              cf_w1, cf_b1, cf_dw, cf_dwb, cf_ln_g, cf_ln_b, cf_w2, cf_b2,
              gla_wq, gla_wk, gla_wv, gla_wr, gla_wg1, gla_wg2, gla_bg, gla_norm_g, gla_wo,
              ffn_wa, ffn_wb, ffn_dw, ffn_dwb, ffn_wo):
    for i in range(DEPTH):
        last = i == DEPTH - 1
        j = i // N_MIXERS
        sh1, sc1, g1, sh2, sc2, g2 = [t[:, None, :] for t in adaln(c, ada_w[i], ada_b[i])]
        csh1, csc1, cg1, csh2, csc2, cg2 = adaln(c_ctx, ada_w[i], ada_b[i])
        hx = modulate(rms_norm(x, norm_pre_mix[i]), sh1, sc1)
        if i % N_MIXERS == 0:
            cf = (cf_w1[j], cf_b1[j], cf_dw[j], cf_dwb[j], cf_ln_g[j], cf_ln_b[j], cf_w2[j], cf_b2[j])
            yx = conformer_conv(hx, *cf)
            yc = None if last else conformer_conv(modulate(rms_norm(ctx, norm_pre_mix[i]), csh1, csc1), *cf)
        else:
            hc = modulate(rms_norm(ctx, norm_pre_mix[i]), csh1, csc1)
            yx, yc = gla_mixer(hx, hc, gla_wq[j], gla_wk[j], gla_wv[j], gla_wr[j], gla_wg1[j], gla_wg2[j],
                               gla_bg[j], gla_norm_g[j], gla_wo[j], ctx_out=not last)
        x = x + g1 * rms_norm(yx, norm_post_mix[i])
        ffn = (ffn_wa[i], ffn_wb[i], ffn_dw[i], ffn_dwb[i], ffn_wo[i])
        hx = modulate(rms_norm(x, norm_pre_ffn[i]), sh2, sc2)
        x = x + g2 * rms_norm(conv_ffn(hx, *ffn, on_grid=True), norm_post_ffn[i])
        if not last:
            ctx = ctx + cg1 * rms_norm(yc, norm_post_mix[i])
            hc = modulate(rms_norm(ctx, norm_pre_ffn[i]), csh2, csc2)
            ctx = ctx + cg2 * rms_norm(conv_ffn(hc, *ffn, on_grid=False), norm_post_ffn[i])
    return x
```

```python
import functools

import jax
import jax.numpy as jnp
from jax import lax
from jax.experimental import pallas as pl
from jax.experimental.pallas import tpu as pltpu

EPS = 1e-6
N_MIXERS = 2
GRID_W = 64
GLA_HEADS = 4
GATE_TAU = 16.0
CHUNK = 64

V7X_LANES = 128
V7X_SUBLANES = 8
V7X_VMEM_BYTES = 64 * 1024 * 1024
VMEM_LIMIT = V7X_VMEM_BYTES * 7 // 8

F32 = jnp.float32
BF16 = jnp.bfloat16

SH1, SC1, G1, SH2, SC2, G2 = range(6)


def _params(*sem):
    return pltpu.CompilerParams(dimension_semantics=sem, vmem_limit_bytes=VMEM_LIMIT)


def _const_spec(shape):
    nd = len(shape)
    return pl.BlockSpec(shape, lambda *_: (0,) * nd, pipeline_mode=pl.Buffered(1))


def _rms(x, g):
    return x * lax.rsqrt(jnp.mean(x * x, axis=-1, keepdims=True) + EPS) * g


def _silu(x):
    return x * jax.nn.sigmoid(x)


def _norm_mod(x, g, mod_ref, shift_row, scale_row):
    h = _rms(x, g)
    return h * (1.0 + mod_ref[0, scale_row:scale_row + 1, :]) + mod_ref[0, shift_row:shift_row + 1, :]


def _residual(x, y, gpost, mod_ref, gate_row):
    return x + mod_ref[0, gate_row:gate_row + 1, :] * _rms(y, gpost)


def _dot(a, b):
    return jnp.dot(a, b, preferred_element_type=F32)


def _dot_nt(a, b):
    return lax.dot_general(a, b, (((1,), (1,)), ((), ())), preferred_element_type=F32)


def _dot_tn(a, b):
    return lax.dot_general(a, b, (((0,), (0,)), ((), ())), preferred_element_type=F32)


def _ada_kernel(cond_ref, w_ref, b_ref, o_ref):
    s = _silu(cond_ref[...])
    o_ref[0] = jnp.dot(s, w_ref[0], preferred_element_type=F32, precision=lax.Precision.HIGHEST) + b_ref[0]


def _ada_call(cond, ada_w, ada_b):
    depth, d, d6 = ada_w.shape
    rows = cond.shape[0]
    tn = d6 // 8
    return pl.pallas_call(
        _ada_kernel,
        grid=(depth, d6 // tn),
        in_specs=[
            pl.BlockSpec((rows, d), lambda i, j: (0, 0)),
            pl.BlockSpec((1, d, tn), lambda i, j: (i, 0, j)),
            pl.BlockSpec((1, 1, tn), lambda i, j: (i, 0, j)),
        ],
        out_specs=pl.BlockSpec((1, rows, tn), lambda i, j: (i, 0, j)),
        out_shape=jax.ShapeDtypeStruct((depth, rows, d6), F32),
        compiler_params=_params("parallel", "parallel"),
        name="ada",
    )(cond, ada_w, ada_b.reshape(depth, 1, d6))


def _cf1_kernel(x_ref, mod_ref, g_ref, w1_ref, b1_ref, o_ref):
    d = x_ref.shape[-1]
    hb = _norm_mod(x_ref[0], g_ref[...], mod_ref, SH1, SC1).astype(BF16)
    a = _dot(hb, w1_ref[:, :d]) + b1_ref[:, :d]
    gt = _dot(hb, w1_ref[:, d:]) + b1_ref[:, d:]
    o_ref[0] = a * jax.nn.sigmoid(gt)


def _cf1_call(x, mods, g, w1, b1, tn):
    bsz, n, d = x.shape
    return pl.pallas_call(
        _cf1_kernel,
        grid=(bsz, n // tn),
        in_specs=[
            pl.BlockSpec((1, tn, d), lambda b, t: (b, t, 0)),
            pl.BlockSpec((1, 6, d), lambda b, t: (b, 0, 0)),
            _const_spec((1, d)),
            _const_spec((d, 2 * d)),
            _const_spec((1, 2 * d)),
        ],
        out_specs=pl.BlockSpec((1, tn, d), lambda b, t: (b, t, 0)),
        out_shape=jax.ShapeDtypeStruct((bsz, n, d), F32),
        compiler_params=_params("parallel", "parallel"),
        name="cf1",
    )(x, mods, g, w1, b1)


CF_HALO = 16
CF_ROWS = 32


def _cf2_kernel(glu_ref, prev_ref, next_ref, dw_ref, dwb_ref, lng_ref, lnb_ref, w2_ref, b2_ref,
                x_ref, mod_ref, gpost_ref, o_ref, buf_ref, u_ref, *, width):
    t = pl.program_id(1)
    nt = pl.num_programs(1)
    tn = glu_ref.shape[1]
    pad = width // 2
    buf_ref[0:CF_HALO] = jnp.where(t > 0, prev_ref[0], 0.0)
    buf_ref[CF_HALO:CF_HALO + tn] = glu_ref[0]
    buf_ref[CF_HALO + tn:] = jnp.where(t < nt - 1, next_ref[0], 0.0)
    for r0 in range(0, tn, CF_ROWS):
        acc = None
        for k in range(width):
            lo = r0 + CF_HALO - pad + k
            term = buf_ref[lo:lo + CF_ROWS, :] * dw_ref[k:k + 1, :]
            acc = term if acc is None else acc + term
        u_ref[r0:r0 + CF_ROWS] = acc
    u = u_ref[...] + dwb_ref[...]
    mu = jnp.mean(u, axis=-1, keepdims=True)
    uc = u - mu
    var = jnp.mean(uc * uc, axis=-1, keepdims=True)
    v = _silu(uc * lax.rsqrt(var + EPS) * lng_ref[...] + lnb_ref[...])
    y = _dot(v.astype(BF16), w2_ref[...]) + b2_ref[...]
    o_ref[0] = _residual(x_ref[0], y, gpost_ref[...], mod_ref, G1)


def _cf2_call(glu, x, mods, dw, dwb, lng, lnb, w2, b2, gpost, tn):
    bsz, n, d = x.shape
    width = dw.shape[0]
    assert width // 2 < CF_HALO and tn % CF_ROWS == 0
    hb = tn // CF_HALO
    last = n // CF_HALO - 1
    return pl.pallas_call(
        functools.partial(_cf2_kernel, width=width),
        grid=(bsz, n // tn),
        in_specs=[
            pl.BlockSpec((1, tn, d), lambda b, t: (b, t, 0)),
            pl.BlockSpec((1, CF_HALO, d), lambda b, t: (b, jnp.maximum(t * hb - 1, 0), 0)),
            pl.BlockSpec((1, CF_HALO, d), lambda b, t: (b, jnp.minimum((t + 1) * hb, last), 0)),
            _const_spec((width, d)),
            _const_spec((1, d)),
            _const_spec((1, d)),
            _const_spec((1, d)),
            _const_spec((d, d)),
            _const_spec((1, d)),
            pl.BlockSpec((1, tn, d), lambda b, t: (b, t, 0)),
            pl.BlockSpec((1, 6, d), lambda b, t: (b, 0, 0)),
            _const_spec((1, d)),
        ],
        out_specs=pl.BlockSpec((1, tn, d), lambda b, t: (b, t, 0)),
        out_shape=jax.ShapeDtypeStruct((bsz, n, d), F32),
        scratch_shapes=[pltpu.VMEM((tn + 2 * CF_HALO, d), F32), pltpu.VMEM((tn, d), F32)],
        compiler_params=_params("parallel", "parallel"),
        name="cf2",
    )(glu, glu, glu, dw, dwb, lng, lnb, w2, b2, x, mods, gpost)


def _ffn1_kernel(x_ref, mod_ref, g_ref, wa_ref, wb_ref, a_ref, b_ref):
    hb = _norm_mod(x_ref[0], g_ref[...], mod_ref, SH2, SC2).astype(BF16)
    a_ref[0] = _dot(hb, wa_ref[...])
    b_ref[0] = _dot(hb, wb_ref[...])


def _ffn1_call(x, mods, g, wa, wb, tn):
    bsz, n, d = x.shape
    f = wa.shape[1]
    return pl.pallas_call(
        _ffn1_kernel,
        grid=(bsz, n // tn),
        in_specs=[
            pl.BlockSpec((1, tn, d), lambda b, t: (b, t, 0)),
            pl.BlockSpec((1, 6, d), lambda b, t: (b, 0, 0)),
            _const_spec((1, d)),
            _const_spec((d, f)),
            _const_spec((d, f)),
        ],
        out_specs=[pl.BlockSpec((1, tn, f), lambda b, t: (b, t, 0)),
                   pl.BlockSpec((1, tn, f), lambda b, t: (b, t, 0))],
        out_shape=[jax.ShapeDtypeStruct((bsz, n, f), F32), jax.ShapeDtypeStruct((bsz, n, f), F32)],
        compiler_params=_params("parallel", "parallel"),
        name="ffn1",
    )(x, mods, g, wa, wb)


FFN_LANES = 2 * V7X_LANES


def _ffn2_kernel(*refs, gw, vertical):
    if vertical:
        a_ref, ap_ref, an_ref, b_ref, dw_ref, dwb_ref, wo_ref, x_ref, mod_ref, gpost_ref, o_ref, gated_ref = refs
    else:
        a_ref, b_ref, dw_ref, dwb_ref, wo_ref, x_ref, mod_ref, gpost_ref, o_ref, gated_ref = refs
    t = pl.program_id(1)
    nt = pl.num_programs(1)
    tn, f = a_ref.shape[1:]
    nrows = tn // gw
    for f0 in range(0, f, FFN_LANES):
        fl = min(FFN_LANES, f - f0)
        fs = slice(f0, f0 + fl)
        col = lax.broadcasted_iota(jnp.int32, (gw, fl), 0)
        w = [dw_ref[k:k + 1, fs] for k in range(dw_ref.shape[0])]
        bias = dwb_ref[:, fs]
        for i in range(nrows):
            rs = slice(i * gw, (i + 1) * gw)
            mid = a_ref[0, rs, fs]
            if vertical:
                if i > 0:
                    up = a_ref[0, (i - 1) * gw:i * gw, fs]
                else:
                    up = jnp.where(t > 0, ap_ref[0, :, fs], 0.0)
                if i < nrows - 1:
                    dn = a_ref[0, (i + 1) * gw:(i + 2) * gw, fs]
                else:
                    dn = jnp.where(t < nt - 1, an_ref[0, :, fs], 0.0)
                left, centre, right = [w[j] * up + w[3 + j] * mid + w[6 + j] * dn for j in range(3)]
            else:
                left, centre, right = [w[j] * mid for j in range(3)]
            conv = (centre + bias
                    + jnp.where(col > 0, pltpu.roll(left, 1, 0), 0.0)
                    + jnp.where(col < gw - 1, pltpu.roll(right, gw - 1, 0), 0.0))
            gated_ref[rs, fs] = (_silu(conv) * b_ref[0, rs, fs]).astype(BF16)
    y = _dot(gated_ref[...], wo_ref[...])
    o_ref[0] = _residual(x_ref[0], y, gpost_ref[...], mod_ref, G2)


def _ffn2_call(a, bgate, x, mods, dw, dwb, wo, gpost, tn, gw, vertical):
    bsz, n, d = x.shape
    f = a.shape[-1]
    assert tn % gw == 0
    hb = tn // gw
    last = n // gw - 1
    tile_f = pl.BlockSpec((1, tn, f), lambda b, t: (b, t, 0))
    tile_d = pl.BlockSpec((1, tn, d), lambda b, t: (b, t, 0))
    in_specs = [tile_f]
    args = [a]
    if vertical:
        in_specs += [pl.BlockSpec((1, gw, f), lambda b, t: (b, jnp.maximum(t * hb - 1, 0), 0)),
                     pl.BlockSpec((1, gw, f), lambda b, t: (b, jnp.minimum((t + 1) * hb, last), 0))]
        args += [a, a]
    in_specs += [tile_f, _const_spec(dw.shape), _const_spec((1, f)), _const_spec((f, d)), tile_d,
                 pl.BlockSpec((1, 6, d), lambda b, t: (b, 0, 0)), _const_spec((1, d))]
    args += [bgate, dw, dwb, wo, x, mods, gpost]
    return pl.pallas_call(
        functools.partial(_ffn2_kernel, gw=gw, vertical=vertical),
        grid=(bsz, n // tn),
        in_specs=in_specs,
        out_specs=tile_d,
        out_shape=jax.ShapeDtypeStruct((bsz, n, d), F32),
        scratch_shapes=[pltpu.VMEM((tn, f), BF16)],
        compiler_params=_params("parallel", "parallel"),
        name="ffn2_grid" if vertical else "ffn2_seq",
    )(*args)


def _log_sigmoid(z):
    return jnp.minimum(z, 0.0) - jnp.log1p(jnp.exp(-jnp.abs(z)))


def _gla_proj_kernel(x_ref, mod_ref, g_ref, wq_ref, wk_ref, wv_ref, wr_ref, wg1_ref, wg2_ref, bg_ref,
                     q_ref, k_ref, v_ref, r_ref, cf_ref, cb_ref, *, q_scale):
    tn = x_ref.shape[1]
    dk = q_ref.shape[-1]
    hb = _norm_mod(x_ref[0], g_ref[...], mod_ref, SH1, SC1).astype(BF16)
    q_ref[0] = _dot(hb, wq_ref[...]) * q_scale
    k_ref[0] = _dot(hb, wk_ref[...])
    v_ref[0] = _dot(hb, wv_ref[...])
    r_ref[0] = _silu(_dot(hb, wr_ref[...]))
    low = _dot(hb, wg1_ref[...]).astype(BF16)
    g = _log_sigmoid(_dot(low, wg2_ref[...]) + bg_ref[...]) * (1.0 / GATE_TAU)
    row = lax.broadcasted_iota(jnp.int32, (CHUNK, CHUNK), 0)
    colm = lax.broadcasted_iota(jnp.int32, (CHUNK, CHUNK), 1)
    tri_f = (colm <= row).astype(F32)
    tri_b = (colm >= row).astype(F32)
    for c0 in range(0, tn, CHUNK):
        cs = slice(c0, c0 + CHUNK)
        cf_ref[0, cs, :] = jnp.dot(tri_f, g[cs, :dk], preferred_element_type=F32, precision=lax.Precision.HIGHEST)
        cb_ref[0, cs, :] = jnp.dot(tri_b, g[cs, dk:], preferred_element_type=F32, precision=lax.Precision.HIGHEST)


def _gla_proj_call(x, mods, g, wq, wk, wv, wr, wg1, wg2, bg, tn):
    bsz, n, d = x.shape
    dk, dv = wq.shape[1], wv.shape[1]
    tile = lambda w: pl.BlockSpec((1, tn, w), lambda b, t: (b, t, 0))
    shp = lambda w: jax.ShapeDtypeStruct((bsz, n, w), F32)
    return pl.pallas_call(
        functools.partial(_gla_proj_kernel, q_scale=float((dk // GLA_HEADS) ** -0.5)),
        grid=(bsz, n // tn),
        in_specs=[tile(d), pl.BlockSpec((1, 6, d), lambda b, t: (b, 0, 0)), _const_spec((1, d)),
                  _const_spec(wq.shape), _const_spec(wk.shape), _const_spec(wv.shape), _const_spec(wr.shape),
                  _const_spec(wg1.shape), _const_spec(wg2.shape), _const_spec(bg.shape)],
        out_specs=[tile(dk), tile(dk), tile(dv), tile(dv), tile(dk), tile(dk)],
        out_shape=[shp(dk), shp(dk), shp(dv), shp(dv), shp(dk), shp(dk)],
        compiler_params=_params("parallel", "parallel"),
        name="gla_proj",
    )(x, mods, g, wq, wk, wv, wr, wg1, wg2, bg)


CHUNK_LEVELS = 6
assert 1 << CHUNK_LEVELS == CHUNK


def _level_reference(c_ref, base, level, reverse):
    half = 1 << level
    blk = 2 * half
    off = half if reverse else half - 1
    dk = c_ref.shape[-1]

    def row(r):
        return c_ref[0, pl.ds(base + (r + off), 1), :]

    if blk >= V7X_SUBLANES:
        return jnp.concatenate([jnp.broadcast_to(row(b0), (blk, dk)) for b0 in range(0, CHUNK, blk)], axis=0)
    sub = lax.broadcasted_iota(jnp.int32, (V7X_SUBLANES, dk), 0)
    groups = []
    for g0 in range(0, CHUNK, V7X_SUBLANES):
        r = jnp.broadcast_to(row(g0), (V7X_SUBLANES, dk))
        for b0 in range(blk, V7X_SUBLANES, blk):
            r = jnp.where(sub >= b0, jnp.broadcast_to(row(g0 + b0), (V7X_SUBLANES, dk)), r)
        groups.append(r)
    return jnp.concatenate(groups, axis=0)


def _gla_scan_kernel(q_ref, k_ref, v_ref, c_ref, s0_ref, o_ref, sfin_ref, st_ref, *, reverse, nchunks):
    j = pl.program_id(2)

    @pl.when(j == 0)
    def _():
        st_ref[...] = s0_ref[0, 0]

    ti = lax.broadcasted_iota(jnp.int32, (CHUNK, CHUNK), 0)
    si = lax.broadcasted_iota(jnp.int32, (CHUNK, CHUNK), 1)
    x = ti ^ si
    lvl = sum([(x >= (1 << b)).astype(jnp.int32) for b in range(1, CHUNK_LEVELS)])
    causal = (ti < si) if reverse else (ti > si)
    end_row = 0 if reverse else CHUNK - 1

    def body(i, carry):
        ci = (nchunks - 1 - i) if reverse else i
        base = pl.multiple_of(ci * CHUNK, CHUNK)
        rows = pl.ds(base, CHUNK)
        q = q_ref[0, rows, :]
        k = k_ref[0, rows, :]
        c = c_ref[0, rows, :]
        vb = v_ref[0, rows, :].astype(BF16)
        st = st_ref[...]
        c_end = c_ref[0, pl.ds(base + end_row, 1), :]
        o = _dot_nt((q * jnp.exp(c)).astype(BF16), st.astype(BF16))
        scores = jnp.where(ti == si, _dot_nt(q.astype(BF16), k.astype(BF16)), 0.0)
        for level in range(CHUNK_LEVELS):
            e = jnp.exp(-jnp.abs(c - _level_reference(c_ref, base, level, reverse)))
            part = _dot_nt((q * e).astype(BF16), (k * e).astype(BF16))
            scores = jnp.where(causal & (lvl == level), part, scores)
        o_ref[0, rows, :] = o + _dot(scores.astype(BF16), vb)
        kd = (k * jnp.exp(c_end - c)).astype(BF16)
        st_ref[...] = st * jnp.exp(c_end) + _dot_tn(vb, kd)
        return carry

    lax.fori_loop(0, nchunks, body, 0)

    @pl.when(j == pl.num_programs(2) - 1)
    def _():
        sfin_ref[0, 0] = st_ref[...]


def _gla_scan_call(q, k, v, c, s0, reverse, nchunks):
    bsz, n, dk = q.shape
    dv = v.shape[-1]
    hk, hv = dk // GLA_HEADS, dv // GLA_HEADS
    tn = nchunks * CHUNK
    nj = n // tn
    seq = (lambda j: nj - 1 - j) if reverse else (lambda j: j)
    tile = lambda w: pl.BlockSpec((1, tn, w), lambda b, h, j: (b, seq(j), h))
    state = pl.BlockSpec((1, 1, hv, hk), lambda b, h, j: (b, h, 0, 0))
    return pl.pallas_call(
        functools.partial(_gla_scan_kernel, reverse=reverse, nchunks=nchunks),
        grid=(bsz, GLA_HEADS, nj),
        in_specs=[tile(hk), tile(hk), tile(hv), tile(hk), state],
        out_specs=[tile(hv), state],
        out_shape=[jax.ShapeDtypeStruct((bsz, n, dv), F32), jax.ShapeDtypeStruct((bsz, GLA_HEADS, hv, hk), F32)],
        scratch_shapes=[pltpu.VMEM((hv, hk), F32)],
        compiler_params=_params("parallel", "parallel", "arbitrary"),
        name="gla_scan_bwd" if reverse else "gla_scan_fwd",
    )(q, k, v, c, s0)


def _gla_out_kernel(of_ref, ob_ref, r_ref, ng_ref, wo_ref, x_ref, mod_ref, gpost_ref, o_ref, *, heads):
    o = of_ref[0] + ob_ref[0]
    hv = o.shape[-1] // heads
    parts = []
    for h in range(heads):
        oh = o[:, h * hv:(h + 1) * hv]
        parts.append(oh * lax.rsqrt(jnp.mean(oh * oh, axis=-1, keepdims=True) + EPS))
    gated = jnp.concatenate(parts, axis=-1) * ng_ref[...] * r_ref[0]
    y = _dot(gated.astype(BF16), wo_ref[...])
    o_ref[0] = _residual(x_ref[0], y, gpost_ref[...], mod_ref, G1)


def _gla_out_call(o_f, o_b, r, ng, wo, x, mods, gpost, tn):
    bsz, n, d = x.shape
    dv = o_f.shape[-1]
    tile_v = pl.BlockSpec((1, tn, dv), lambda b, t: (b, t, 0))
    tile_d = pl.BlockSpec((1, tn, d), lambda b, t: (b, t, 0))
    return pl.pallas_call(
        functools.partial(_gla_out_kernel, heads=GLA_HEADS),
        grid=(bsz, n // tn),
        in_specs=[tile_v, tile_v, tile_v, _const_spec((1, dv)), _const_spec((dv, d)), tile_d,
                  pl.BlockSpec((1, 6, d), lambda b, t: (b, 0, 0)), _const_spec((1, d))],
        out_specs=tile_d,
        out_shape=jax.ShapeDtypeStruct((bsz, n, d), F32),
        compiler_params=_params("parallel", "parallel"),
        name="gla_out",
    )(o_f, o_b, r, ng, wo, x, mods, gpost)


def _tile(n, want):
    return min(n, want)


def _conformer(x, mods, g_pre, g_post, w):
    tn = _tile(x.shape[1], 512)
    glu = _cf1_call(x, mods, g_pre, w["w1"], w["b1"], tn)
    return _cf2_call(glu, x, mods, w["dw"], w["dwb"], w["ln_g"], w["ln_b"], w["w2"], w["b2"], g_post, tn)


def _gla_states(h, mods, g_pre, w, s_f, s_b):
    n = h.shape[1]
    q, k, v, r, cf, cb = _gla_proj_call(h, mods, g_pre, w["wq"], w["wk"], w["wv"], w["wr"], w["wg1"], w["wg2"],
                                        w["bg"], _tile(n, 512))
    nchunks = _tile(n, 1024) // CHUNK
    o_f, fin_f = _gla_scan_call(q, k, v, cf, s_f, False, nchunks)
    o_b, fin_b = _gla_scan_call(q, k, v, cb, s_b, True, nchunks)
    return o_f, o_b, r, fin_f, fin_b


def _gla_readout(x, mods, g_post, w, o_f, o_b, r):
    return _gla_out_call(o_f, o_b, r, w["norm_g"], w["wo"], x, mods, g_post, _tile(x.shape[1], 512))


def _conv_ffn(x, mods, g_pre, g_post, w, on_grid):
    n = x.shape[1]
    tn = _tile(n, 256)
    a, bgate = _ffn1_call(x, mods, g_pre, w["wa"], w["wb"], tn)
    if on_grid:
        return _ffn2_call(a, bgate, x, mods, w["dw"], w["dwb"], w["wo"], g_post, tn, GRID_W, True)
    return _ffn2_call(a, bgate, x, mods, w["dw"][3:6], w["dwb"], w["wo"], g_post, n, n, False)


def kernel(x, c, ctx, c_ctx, ada_w, ada_b, norm_pre_mix, norm_post_mix, norm_pre_ffn, norm_post_ffn, cf_w1, cf_b1, cf_dw, cf_dwb, cf_ln_g, cf_ln_b, cf_w2, cf_b2, gla_wq, gla_wk, gla_wv, gla_wr, gla_wg1, gla_wg2, gla_bg, gla_norm_g, gla_wo, ffn_wa, ffn_wb, ffn_dw, ffn_dwb, ffn_wo):
    bsz, n, d = x.shape
    depth = ada_w.shape[0]
    dk = gla_wq.shape[-1]
    dv = gla_wv.shape[-1]
    hk, hv = dk // GLA_HEADS, dv // GLA_HEADS
    row = lambda v: v.reshape(1, -1)

    cond_rows = -(-(bsz + 1) // V7X_SUBLANES) * V7X_SUBLANES
    cond = jnp.zeros((cond_rows, d), F32).at[:bsz].set(c).at[bsz].set(c_ctx)
    mods = _ada_call(cond, ada_w, ada_b)

    for i in range(depth):
        last = i == depth - 1
        j = i // N_MIXERS
        mx = mods[i, :bsz].reshape(bsz, 6, d)
        mc = jnp.broadcast_to(mods[i, bsz].reshape(1, 6, d), (bsz, 6, d))
        g_pre, g_post = row(norm_pre_mix[i]), row(norm_post_mix[i])
        if i % N_MIXERS == 0:
            w = dict(w1=cf_w1[j].astype(BF16), b1=row(cf_b1[j]), dw=cf_dw[j], dwb=row(cf_dwb[j]),
                     ln_g=row(cf_ln_g[j]), ln_b=row(cf_ln_b[j]), w2=cf_w2[j].astype(BF16), b2=row(cf_b2[j]))
            x = _conformer(x, mx, g_pre, g_post, w)
            if not last:
                ctx_mixed = _conformer(ctx, mc, g_pre, g_post, w)
        else:
            rank = gla_wg1.shape[-1]
            wg2 = jnp.zeros((2 * rank, 2 * dk), F32)
            wg2 = wg2.at[:rank, :dk].set(gla_wg2[j, 0]).at[rank:, dk:].set(gla_wg2[j, 1])
            w = dict(wq=gla_wq[j].astype(BF16), wk=gla_wk[j].astype(BF16), wv=gla_wv[j].astype(BF16),
                     wr=gla_wr[j].astype(BF16),
                     wg1=jnp.concatenate([gla_wg1[j, 0], gla_wg1[j, 1]], axis=-1).astype(BF16),
                     wg2=wg2.astype(BF16), bg=gla_bg[j].reshape(1, 2 * dk),
                     norm_g=jnp.tile(gla_norm_g[j], GLA_HEADS).reshape(1, dv), wo=gla_wo[j].astype(BF16))
            zeros = jnp.zeros((bsz, GLA_HEADS, hv, hk), F32)
            oc_f, oc_b, rc, s_f, s_b = _gla_states(ctx, mc, g_pre, w, zeros, zeros)
            if not last:
                ctx_mixed = _gla_readout(ctx, mc, g_post, w, oc_f, oc_b, rc)
            ox_f, ox_b, rx, _, _ = _gla_states(x, mx, g_pre, w, s_f, s_b)
            x = _gla_readout(x, mx, g_post, w, ox_f, ox_b, rx)
        fw = dict(wa=ffn_wa[i].astype(BF16), wb=ffn_wb[i].astype(BF16), dw=ffn_dw[i].reshape(-1, ffn_dw.shape[-1]),
                  dwb=row(ffn_dwb[i]), wo=ffn_wo[i].astype(BF16))
        g_pre_f, g_post_f = row(norm_pre_ffn[i]), row(norm_post_ffn[i])
        x = _conv_ffn(x, mx, g_pre_f, g_post_f, fw, True)
        if not last:
            ctx = _conv_ffn(ctx_mixed, mc, g_pre_f, g_post_f, fw, False)
    return x
```

```python
import functools

import jax
import jax.numpy as jnp
from jax import lax
from jax.experimental import pallas as pl
from jax.experimental.pallas import tpu as pltpu

EPS = 1e-6
N_MIXERS = 2
GRID_W = 64
GLA_HEADS = 4
GATE_TAU = 16.0
CHUNK = 64
LOG2_E = 1.4426950408889634

V7X_LANES = 128
V7X_SUBLANES = 8
V7X_VMEM_BYTES = 64 * 1024 * 1024
VMEM_LIMIT = V7X_VMEM_BYTES * 7 // 8

F32 = jnp.float32
BF16 = jnp.bfloat16

SH1, SC1, G1, SH2, SC2, G2 = range(6)


def _params(*sem):
    return pltpu.CompilerParams(dimension_semantics=sem, vmem_limit_bytes=VMEM_LIMIT)


def _const_spec(shape):
    nd = len(shape)
    return pl.BlockSpec(shape, lambda *_: (0,) * nd, pipeline_mode=pl.Buffered(1))


def _rms(x, g):
    return x * lax.rsqrt(jnp.mean(x * x, axis=-1, keepdims=True) + EPS) * g


def _silu(x):
    return x * jax.nn.sigmoid(x)


def _norm_mod(x, g, mod_ref, shift_row, scale_row):
    h = _rms(x, g)
    return h * (1.0 + mod_ref[0, scale_row:scale_row + 1, :]) + mod_ref[0, shift_row:shift_row + 1, :]


def _residual(x, y, gpost, mod_ref, gate_row):
    return x + mod_ref[0, gate_row:gate_row + 1, :] * _rms(y, gpost)


def _dot(a, b):
    return jnp.dot(a, b, preferred_element_type=F32)


def _dot_nt(a, b):
    return lax.dot_general(a, b, (((1,), (1,)), ((), ())), preferred_element_type=F32)


def _dot_tn(a, b):
    return lax.dot_general(a, b, (((0,), (0,)), ((), ())), preferred_element_type=F32)


def _ada_kernel(cond_ref, w_ref, b_ref, o_ref):
    s = _silu(cond_ref[...])
    o_ref[0] = jnp.dot(s, w_ref[0], preferred_element_type=F32, precision=lax.Precision.HIGHEST) + b_ref[0]


def _ada_call(cond, ada_w, ada_b):
    depth, d, d6 = ada_w.shape
    rows = cond.shape[0]
    tn = d6 // 8
    return pl.pallas_call(
        _ada_kernel,
        grid=(depth, d6 // tn),
        in_specs=[
            pl.BlockSpec((rows, d), lambda i, j: (0, 0)),
            pl.BlockSpec((1, d, tn), lambda i, j: (i, 0, j)),
            pl.BlockSpec((1, 1, tn), lambda i, j: (i, 0, j)),
        ],
        out_specs=pl.BlockSpec((1, rows, tn), lambda i, j: (i, 0, j)),
        out_shape=jax.ShapeDtypeStruct((depth, rows, d6), F32),
        compiler_params=_params("parallel", "parallel"),
        name="ada",
    )(cond, ada_w, ada_b.reshape(depth, 1, d6))


def _cf1_kernel(x_ref, mod_ref, g_ref, w1_ref, b1_ref, o_ref):
    d = x_ref.shape[-1]
    hb = _norm_mod(x_ref[0], g_ref[...], mod_ref, SH1, SC1).astype(BF16)
    a = _dot(hb, w1_ref[:, :d]) + b1_ref[:, :d]
    gt = _dot(hb, w1_ref[:, d:]) + b1_ref[:, d:]
    o_ref[0] = a * jax.nn.sigmoid(gt)


def _cf1_call(x, mods, g, w1, b1, tn):
    bsz, n, d = x.shape
    return pl.pallas_call(
        _cf1_kernel,
        grid=(bsz, n // tn),
        in_specs=[
            pl.BlockSpec((1, tn, d), lambda b, t: (b, t, 0)),
            pl.BlockSpec((1, 6, d), lambda b, t: (b, 0, 0)),
            _const_spec((1, d)),
            _const_spec((d, 2 * d)),
            _const_spec((1, 2 * d)),
        ],
        out_specs=pl.BlockSpec((1, tn, d), lambda b, t: (b, t, 0)),
        out_shape=jax.ShapeDtypeStruct((bsz, n, d), F32),
        compiler_params=_params("parallel", "parallel"),
        name="cf1",
    )(x, mods, g, w1, b1)


CF_HALO = 16
CF_ROWS = 32


def _cf2_kernel(glu_ref, prev_ref, next_ref, dw_ref, dwb_ref, lng_ref, lnb_ref, w2_ref, b2_ref,
                x_ref, mod_ref, gpost_ref, o_ref, buf_ref, u_ref, *, width):
    t = pl.program_id(1)
    nt = pl.num_programs(1)
    tn = glu_ref.shape[1]
    pad = width // 2
    buf_ref[0:CF_HALO] = jnp.where(t > 0, prev_ref[0], 0.0)
    buf_ref[CF_HALO:CF_HALO + tn] = glu_ref[0]
    buf_ref[CF_HALO + tn:] = jnp.where(t < nt - 1, next_ref[0], 0.0)
    for r0 in range(0, tn, CF_ROWS):
        acc = None
        for k in range(width):
            lo = r0 + CF_HALO - pad + k
            term = buf_ref[lo:lo + CF_ROWS, :] * dw_ref[k:k + 1, :]
            acc = term if acc is None else acc + term
        u_ref[r0:r0 + CF_ROWS] = acc
    u = u_ref[...] + dwb_ref[...]
    mu = jnp.mean(u, axis=-1, keepdims=True)
    uc = u - mu
    var = jnp.mean(uc * uc, axis=-1, keepdims=True)
    v = _silu(uc * lax.rsqrt(var + EPS) * lng_ref[...] + lnb_ref[...])
    y = _dot(v.astype(BF16), w2_ref[...]) + b2_ref[...]
    o_ref[0] = _residual(x_ref[0], y, gpost_ref[...], mod_ref, G1)


def _cf2_call(glu, x, mods, dw, dwb, lng, lnb, w2, b2, gpost, tn):
    bsz, n, d = x.shape
    width = dw.shape[0]
    assert width // 2 < CF_HALO and tn % CF_ROWS == 0
    hb = tn // CF_HALO
    last = n // CF_HALO - 1
    return pl.pallas_call(
        functools.partial(_cf2_kernel, width=width),
        grid=(bsz, n // tn),
        in_specs=[
            pl.BlockSpec((1, tn, d), lambda b, t: (b, t, 0)),
            pl.BlockSpec((1, CF_HALO, d), lambda b, t: (b, jnp.maximum(t * hb - 1, 0), 0)),
            pl.BlockSpec((1, CF_HALO, d), lambda b, t: (b, jnp.minimum((t + 1) * hb, last), 0)),
            _const_spec((width, d)),
            _const_spec((1, d)),
            _const_spec((1, d)),
            _const_spec((1, d)),
            _const_spec((d, d)),
            _const_spec((1, d)),
            pl.BlockSpec((1, tn, d), lambda b, t: (b, t, 0)),
            pl.BlockSpec((1, 6, d), lambda b, t: (b, 0, 0)),
            _const_spec((1, d)),
        ],
        out_specs=pl.BlockSpec((1, tn, d), lambda b, t: (b, t, 0)),
        out_shape=jax.ShapeDtypeStruct((bsz, n, d), F32),
        scratch_shapes=[pltpu.VMEM((tn + 2 * CF_HALO, d), F32), pltpu.VMEM((tn, d), F32)],
        compiler_params=_params("parallel", "parallel"),
        name="cf2",
    )(glu, glu, glu, dw, dwb, lng, lnb, w2, b2, x, mods, gpost)


def _ffn1_kernel(x_ref, mod_ref, g_ref, wa_ref, wb_ref, a_ref, b_ref):
    hb = _norm_mod(x_ref[0], g_ref[...], mod_ref, SH2, SC2).astype(BF16)
    a_ref[0] = _dot(hb, wa_ref[...])
    b_ref[0] = _dot(hb, wb_ref[...])


def _ffn1_call(x, mods, g, wa, wb, tn):
    bsz, n, d = x.shape
    f = wa.shape[1]
    return pl.pallas_call(
        _ffn1_kernel,
        grid=(bsz, n // tn),
        in_specs=[
            pl.BlockSpec((1, tn, d), lambda b, t: (b, t, 0)),
            pl.BlockSpec((1, 6, d), lambda b, t: (b, 0, 0)),
            _const_spec((1, d)),
            _const_spec((d, f)),
            _const_spec((d, f)),
        ],
        out_specs=[pl.BlockSpec((1, tn, f), lambda b, t: (b, t, 0)),
                   pl.BlockSpec((1, tn, f), lambda b, t: (b, t, 0))],
        out_shape=[jax.ShapeDtypeStruct((bsz, n, f), F32), jax.ShapeDtypeStruct((bsz, n, f), F32)],
        compiler_params=_params("parallel", "parallel"),
        name="ffn1",
    )(x, mods, g, wa, wb)


FFN_LANES = 2 * V7X_LANES


def _ffn2_kernel(*refs, gw, vertical):
    if vertical:
        a_ref, ap_ref, an_ref, b_ref, dw_ref, dwb_ref, wo_ref, x_ref, mod_ref, gpost_ref, o_ref, gated_ref = refs
    else:
        a_ref, b_ref, dw_ref, dwb_ref, wo_ref, x_ref, mod_ref, gpost_ref, o_ref, gated_ref = refs
    t = pl.program_id(1)
    nt = pl.num_programs(1)
    tn, f = a_ref.shape[1:]
    nrows = tn // gw
    for f0 in range(0, f, FFN_LANES):
        fl = min(FFN_LANES, f - f0)
        fs = slice(f0, f0 + fl)
        col = lax.broadcasted_iota(jnp.int32, (gw, fl), 0)
        w = [dw_ref[k:k + 1, fs] for k in range(dw_ref.shape[0])]
        bias = dwb_ref[:, fs]
        for i in range(nrows):
            rs = slice(i * gw, (i + 1) * gw)
            mid = a_ref[0, rs, fs]
            if vertical:
                if i > 0:
                    up = a_ref[0, (i - 1) * gw:i * gw, fs]
                else:
                    up = jnp.where(t > 0, ap_ref[0, :, fs], 0.0)
                if i < nrows - 1:
                    dn = a_ref[0, (i + 1) * gw:(i + 2) * gw, fs]
                else:
                    dn = jnp.where(t < nt - 1, an_ref[0, :, fs], 0.0)
                left, centre, right = [w[j] * up + w[3 + j] * mid + w[6 + j] * dn for j in range(3)]
            else:
                left, centre, right = [w[j] * mid for j in range(3)]
            conv = (centre + bias
                    + jnp.where(col > 0, pltpu.roll(left, 1, 0), 0.0)
                    + jnp.where(col < gw - 1, pltpu.roll(right, gw - 1, 0), 0.0))
            gated_ref[rs, fs] = (_silu(conv) * b_ref[0, rs, fs]).astype(BF16)
    y = _dot(gated_ref[...], wo_ref[...])
    o_ref[0] = _residual(x_ref[0], y, gpost_ref[...], mod_ref, G2)


def _ffn2_call(a, bgate, x, mods, dw, dwb, wo, gpost, tn, gw, vertical):
    bsz, n, d = x.shape
    f = a.shape[-1]
    assert tn % gw == 0
    hb = tn // gw
    last = n // gw - 1
    tile_f = pl.BlockSpec((1, tn, f), lambda b, t: (b, t, 0))
    tile_d = pl.BlockSpec((1, tn, d), lambda b, t: (b, t, 0))
    in_specs = [tile_f]
    args = [a]
    if vertical:
        in_specs += [pl.BlockSpec((1, gw, f), lambda b, t: (b, jnp.maximum(t * hb - 1, 0), 0)),
                     pl.BlockSpec((1, gw, f), lambda b, t: (b, jnp.minimum((t + 1) * hb, last), 0))]
        args += [a, a]
    in_specs += [tile_f, _const_spec(dw.shape), _const_spec((1, f)), _const_spec((f, d)), tile_d,
                 pl.BlockSpec((1, 6, d), lambda b, t: (b, 0, 0)), _const_spec((1, d))]
    args += [bgate, dw, dwb, wo, x, mods, gpost]
    return pl.pallas_call(
        functools.partial(_ffn2_kernel, gw=gw, vertical=vertical),
        grid=(bsz, n // tn),
        in_specs=in_specs,
        out_specs=tile_d,
        out_shape=jax.ShapeDtypeStruct((bsz, n, d), F32),
        scratch_shapes=[pltpu.VMEM((tn, f), BF16)],
        compiler_params=_params("parallel", "parallel"),
        name="ffn2_grid" if vertical else "ffn2_seq",
    )(*args)


def _log_sigmoid(z):
    return jnp.minimum(z, 0.0) - jnp.log1p(jnp.exp(-jnp.abs(z)))


def _gla_proj_kernel(x_ref, mod_ref, g_ref, wq_ref, wk_ref, wv_ref, wr_ref, wg1_ref, wg2_ref, bg_ref,
                     q_ref, k_ref, v_ref, r_ref, cf_ref, cb_ref, *, q_scale):
    tn = x_ref.shape[1]
    dk = q_ref.shape[-1]
    hb = _norm_mod(x_ref[0], g_ref[...], mod_ref, SH1, SC1).astype(BF16)
    q_ref[0] = _dot(hb, wq_ref[...]) * q_scale
    k_ref[0] = _dot(hb, wk_ref[...])
    v_ref[0] = _dot(hb, wv_ref[...])
    r_ref[0] = _silu(_dot(hb, wr_ref[...]))
    low = _dot(hb, wg1_ref[...]).astype(BF16)
    g = _log_sigmoid(_dot(low, wg2_ref[...]) + bg_ref[...]) * (LOG2_E / GATE_TAU)
    row = lax.broadcasted_iota(jnp.int32, (CHUNK, CHUNK), 0)
    colm = lax.broadcasted_iota(jnp.int32, (CHUNK, CHUNK), 1)
    tri_f = (colm <= row).astype(F32)
    tri_b = (colm >= row).astype(F32)
    for c0 in range(0, tn, CHUNK):
        cs = slice(c0, c0 + CHUNK)
        cf_ref[0, cs, :] = jnp.dot(tri_f, g[cs, :dk], preferred_element_type=F32, precision=lax.Precision.HIGHEST)
        cb_ref[0, cs, :] = jnp.dot(tri_b, g[cs, dk:], preferred_element_type=F32, precision=lax.Precision.HIGHEST)


def _gla_proj_call(x, mods, g, wq, wk, wv, wr, wg1, wg2, bg, tn):
    bsz, n, d = x.shape
    dk, dv = wq.shape[1], wv.shape[1]
    tile = lambda w: pl.BlockSpec((1, tn, w), lambda b, t: (b, t, 0))
    shp = lambda w: jax.ShapeDtypeStruct((bsz, n, w), F32)
    return pl.pallas_call(
        functools.partial(_gla_proj_kernel, q_scale=float((dk // GLA_HEADS) ** -0.5)),
        grid=(bsz, n // tn),
        in_specs=[tile(d), pl.BlockSpec((1, 6, d), lambda b, t: (b, 0, 0)), _const_spec((1, d)),
                  _const_spec(wq.shape), _const_spec(wk.shape), _const_spec(wv.shape), _const_spec(wr.shape),
                  _const_spec(wg1.shape), _const_spec(wg2.shape), _const_spec(bg.shape)],
        out_specs=[tile(dk), tile(dk), tile(dv), tile(dv), tile(dk), tile(dk)],
        out_shape=[shp(dk), shp(dk), shp(dv), shp(dv), shp(dk), shp(dk)],
        compiler_params=_params("parallel", "parallel"),
        name="gla_proj",
    )(x, mods, g, wq, wk, wv, wr, wg1, wg2, bg)


CHUNK_LEVELS = 6
assert 1 << CHUNK_LEVELS == CHUNK


def _level_reference(c_ref, base, level, reverse):
    half = 1 << level
    blk = 2 * half
    off = half if reverse else half - 1
    dk = c_ref.shape[-1]

    def row(r):
        return c_ref[0, pl.ds(base + (r + off), 1), :]

    if blk >= V7X_SUBLANES:
        return jnp.concatenate([jnp.broadcast_to(row(b0), (blk, dk)) for b0 in range(0, CHUNK, blk)], axis=0)
    sub = lax.broadcasted_iota(jnp.int32, (V7X_SUBLANES, dk), 0)
    groups = []
    for g0 in range(0, CHUNK, V7X_SUBLANES):
        r = jnp.broadcast_to(row(g0), (V7X_SUBLANES, dk))
        for b0 in range(blk, V7X_SUBLANES, blk):
            r = jnp.where(sub >= b0, jnp.broadcast_to(row(g0 + b0), (V7X_SUBLANES, dk)), r)
        groups.append(r)
    return jnp.concatenate(groups, axis=0)


def _gla_scan_kernel(q_ref, k_ref, v_ref, cf_ref, cb_ref, sf0_ref, sb0_ref, o_ref, sff_ref, sbf_ref,
                     stf_ref, stb_ref, *, nchunks):
    stf_ref[...] = sf0_ref[0, 0]
    stb_ref[...] = sb0_ref[0, 0]
    dk = q_ref.shape[-1]
    ti = lax.broadcasted_iota(jnp.int32, (CHUNK, CHUNK), 0)
    si = lax.broadcasted_iota(jnp.int32, (CHUNK, CHUNK), 1)
    x = ti ^ si
    lvl = sum([(x >= (1 << b)).astype(jnp.int32) for b in range(CHUNK_LEVELS)]) - 1
    row = lax.broadcasted_iota(jnp.int32, (CHUNK, dk), 0)
    below_split = [((row >> level) & 1) == 1 for level in range(CHUNK_LEVELS)]

    def intra(base, q, k, v, vb, cf, cb):
        kb = k.astype(BF16)
        scores = None
        for level in range(CHUNK_LEVELS):
            half = 1 << level
            if level == 0:
                xq = jnp.where(below_split[0], cf - pltpu.roll(cf, 1, 0), cb - pltpu.roll(cb, CHUNK - 1, 0))
                part = _dot_nt((q * jnp.exp2(xq)).astype(BF16), kb)
                scores = jnp.where(lvl == 0, part, 0.0)
                continue
            if half >= V7X_SUBLANES:
                xq, xk = [], []
                for b0 in range(0, CHUNK, 2 * half):
                    above, below = slice(b0, b0 + half), slice(b0 + half, b0 + 2 * half)
                    rf = cf_ref[0, pl.ds(base + (b0 + half - 1), 1), :]
                    rb = cb_ref[0, pl.ds(base + (b0 + half), 1), :]
                    xq += [cb[above] - rb, cf[below] - rf]
                    xk += [rf - cf[above], rb - cb[below]]
                xq, xk = jnp.concatenate(xq, axis=0), jnp.concatenate(xk, axis=0)
            else:
                df = cf - _level_reference(cf_ref, base, level, False)
                db = cb - _level_reference(cb_ref, base, level, True)
                xq = jnp.where(below_split[level], df, db)
                xk = -jnp.where(below_split[level], db, df)
            part = _dot_nt((q * jnp.exp2(xq)).astype(BF16), (k * jnp.exp2(xk)).astype(BF16))
            scores = jnp.where(lvl == level, part, scores)
        diag = jnp.sum(q * k, axis=-1, keepdims=True)
        return _dot(scores.astype(BF16), vb) + (2.0 * diag) * v

    def inter(q, k, vb, c, c_end, st_ref):
        st = st_ref[...]
        o = _dot((q * jnp.exp2(c)).astype(BF16), st.astype(BF16))
        kd = (k * jnp.exp2(c_end - c)).astype(BF16)
        decay = jnp.broadcast_to(jnp.exp2(c_end), (dk, dk)).T[:, 0:1]
        st_ref[...] = st * decay + _dot_tn(kd, vb)
        return o

    def step(i, first):
        fbase = pl.multiple_of(i * CHUNK, CHUNK)
        bbase = pl.multiple_of((nchunks - 1 - i) * CHUNK, CHUNK)
        frows, brows = pl.ds(fbase, CHUNK), pl.ds(bbase, CHUNK)
        q, k, v = q_ref[0, frows, :], k_ref[0, frows, :], v_ref[0, frows, :]
        vb = v.astype(BF16)
        cf, cb = cf_ref[0, frows, :], cb_ref[0, frows, :]
        o_f = (intra(fbase, q, k, v, vb, cf, cb)
               + inter(q, k, vb, cf, cf_ref[0, pl.ds(fbase + CHUNK - 1, 1), :], stf_ref))
        o_b = inter(q_ref[0, brows, :], k_ref[0, brows, :], v_ref[0, brows, :].astype(BF16), cb_ref[0, brows, :],
                    cb_ref[0, pl.ds(bbase, 1), :], stb_ref)
        if first:
            o_ref[0, frows, :] = o_f
            o_ref[0, brows, :] = o_b
        else:
            o_ref[0, frows, :] += o_f
            o_ref[0, brows, :] += o_b

    lax.fori_loop(0, nchunks // 2, lambda i, carry: (step(i, True), carry)[1], 0, unroll=2)
    lax.fori_loop(nchunks // 2, nchunks, lambda i, carry: (step(i, False), carry)[1], 0, unroll=2)
    sff_ref[0, 0] = stf_ref[...]
    sbf_ref[0, 0] = stb_ref[...]


def _gla_scan_call(q, k, v, cf, cb, s_f, s_b):
    bsz, n, dk = q.shape
    dv = v.shape[-1]
    hk, hv = dk // GLA_HEADS, dv // GLA_HEADS
    nchunks = n // CHUNK
    assert nchunks % 2 == 0
    tile = lambda w: pl.BlockSpec((1, n, w), lambda b, h: (b, 0, h))
    state = pl.BlockSpec((1, 1, hk, hv), lambda b, h: (b, h, 0, 0))
    state_shape = jax.ShapeDtypeStruct((bsz, GLA_HEADS, hk, hv), F32)
    return pl.pallas_call(
        functools.partial(_gla_scan_kernel, nchunks=nchunks),
        grid=(bsz, GLA_HEADS),
        in_specs=[tile(hk), tile(hk), tile(hv), tile(hk), tile(hk), state, state],
        out_specs=[tile(hv), state, state],
        out_shape=[jax.ShapeDtypeStruct((bsz, n, dv), F32), state_shape, state_shape],
        scratch_shapes=[pltpu.VMEM((hk, hv), F32), pltpu.VMEM((hk, hv), F32)],
        compiler_params=_params("parallel", "parallel"),
        name="gla_scan",
    )(q, k, v, cf, cb, s_f, s_b)


def _gla_out_kernel(o_in_ref, r_ref, ng_ref, wo_ref, x_ref, mod_ref, gpost_ref, o_ref, *, heads):
    o = o_in_ref[0]
    hv = o.shape[-1] // heads
    parts = []
    for h in range(heads):
        oh = o[:, h * hv:(h + 1) * hv]
        parts.append(oh * lax.rsqrt(jnp.mean(oh * oh, axis=-1, keepdims=True) + EPS))
    gated = jnp.concatenate(parts, axis=-1) * ng_ref[...] * r_ref[0]
    y = _dot(gated.astype(BF16), wo_ref[...])
    o_ref[0] = _residual(x_ref[0], y, gpost_ref[...], mod_ref, G1)


def _gla_out_call(o, r, ng, wo, x, mods, gpost, tn):
    bsz, n, d = x.shape
    dv = o.shape[-1]
    tile_v = pl.BlockSpec((1, tn, dv), lambda b, t: (b, t, 0))
    tile_d = pl.BlockSpec((1, tn, d), lambda b, t: (b, t, 0))
    return pl.pallas_call(
        functools.partial(_gla_out_kernel, heads=GLA_HEADS),
        grid=(bsz, n // tn),
        in_specs=[tile_v, tile_v, _const_spec((1, dv)), _const_spec((dv, d)), tile_d,
                  pl.BlockSpec((1, 6, d), lambda b, t: (b, 0, 0)), _const_spec((1, d))],
        out_specs=tile_d,
        out_shape=jax.ShapeDtypeStruct((bsz, n, d), F32),
        compiler_params=_params("parallel", "parallel"),
        name="gla_out",
    )(o, r, ng, wo, x, mods, gpost)


def _tile(n, want):
    return min(n, want)


def _conformer(x, mods, g_pre, g_post, w):
    tn = _tile(x.shape[1], 512)
    glu = _cf1_call(x, mods, g_pre, w["w1"], w["b1"], tn)
    return _cf2_call(glu, x, mods, w["dw"], w["dwb"], w["ln_g"], w["ln_b"], w["w2"], w["b2"], g_post, tn)


def _gla_states(h, mods, g_pre, w, s_f, s_b):
    n = h.shape[1]
    q, k, v, r, cf, cb = _gla_proj_call(h, mods, g_pre, w["wq"], w["wk"], w["wv"], w["wr"], w["wg1"], w["wg2"],
                                        w["bg"], _tile(n, 512))
    o, fin_f, fin_b = _gla_scan_call(q, k, v, cf, cb, s_f, s_b)
    return o, r, fin_f, fin_b


def _gla_readout(x, mods, g_post, w, o, r):
    return _gla_out_call(o, r, w["norm_g"], w["wo"], x, mods, g_post, _tile(x.shape[1], 512))


def _conv_ffn(x, mods, g_pre, g_post, w, on_grid):
    n = x.shape[1]
    tn = _tile(n, 256)
    a, bgate = _ffn1_call(x, mods, g_pre, w["wa"], w["wb"], tn)
    if on_grid:
        return _ffn2_call(a, bgate, x, mods, w["dw"], w["dwb"], w["wo"], g_post, tn, GRID_W, True)
    return _ffn2_call(a, bgate, x, mods, w["dw"][3:6], w["dwb"], w["wo"], g_post, n, n, False)


def kernel(x, c, ctx, c_ctx, ada_w, ada_b, norm_pre_mix, norm_post_mix, norm_pre_ffn, norm_post_ffn, cf_w1, cf_b1, cf_dw, cf_dwb, cf_ln_g, cf_ln_b, cf_w2, cf_b2, gla_wq, gla_wk, gla_wv, gla_wr, gla_wg1, gla_wg2, gla_bg, gla_norm_g, gla_wo, ffn_wa, ffn_wb, ffn_dw, ffn_dwb, ffn_wo):
    bsz, n, d = x.shape
    depth = ada_w.shape[0]
    dk = gla_wq.shape[-1]
    dv = gla_wv.shape[-1]
    hk, hv = dk // GLA_HEADS, dv // GLA_HEADS
    row = lambda v: v.reshape(1, -1)

    cond_rows = -(-(bsz + 1) // V7X_SUBLANES) * V7X_SUBLANES
    cond = jnp.zeros((cond_rows, d), F32).at[:bsz].set(c).at[bsz].set(c_ctx)
    mods = _ada_call(cond, ada_w, ada_b)

    for i in range(depth):
        last = i == depth - 1
        j = i // N_MIXERS
        mx = mods[i, :bsz].reshape(bsz, 6, d)
        mc = jnp.broadcast_to(mods[i, bsz].reshape(1, 6, d), (bsz, 6, d))
        g_pre, g_post = row(norm_pre_mix[i]), row(norm_post_mix[i])
        if i % N_MIXERS == 0:
            w = dict(w1=cf_w1[j].astype(BF16), b1=row(cf_b1[j]), dw=cf_dw[j], dwb=row(cf_dwb[j]),
                     ln_g=row(cf_ln_g[j]), ln_b=row(cf_ln_b[j]), w2=cf_w2[j].astype(BF16), b2=row(cf_b2[j]))
            x = _conformer(x, mx, g_pre, g_post, w)
            if not last:
                ctx_mixed = _conformer(ctx, mc, g_pre, g_post, w)
        else:
            rank = gla_wg1.shape[-1]
            wg2 = jnp.zeros((2 * rank, 2 * dk), F32)
            wg2 = wg2.at[:rank, :dk].set(gla_wg2[j, 0]).at[rank:, dk:].set(gla_wg2[j, 1])
            w = dict(wq=gla_wq[j].astype(BF16), wk=gla_wk[j].astype(BF16), wv=gla_wv[j].astype(BF16),
                     wr=gla_wr[j].astype(BF16),
                     wg1=jnp.concatenate([gla_wg1[j, 0], gla_wg1[j, 1]], axis=-1).astype(BF16),
                     wg2=wg2.astype(BF16), bg=gla_bg[j].reshape(1, 2 * dk),
                     norm_g=jnp.tile(gla_norm_g[j], GLA_HEADS).reshape(1, dv), wo=gla_wo[j].astype(BF16))
            zeros = jnp.zeros((bsz, GLA_HEADS, hk, hv), F32)
            oc, rc, s_f, s_b = _gla_states(ctx, mc, g_pre, w, zeros, zeros)
            if not last:
                ctx_mixed = _gla_readout(ctx, mc, g_post, w, oc, rc)
            ox, rx, _, _ = _gla_states(x, mx, g_pre, w, s_f, s_b)
            x = _gla_readout(x, mx, g_post, w, ox, rx)
        fw = dict(wa=ffn_wa[i].astype(BF16), wb=ffn_wb[i].astype(BF16), dw=ffn_dw[i].reshape(-1, ffn_dw.shape[-1]),
                  dwb=row(ffn_dwb[i]), wo=ffn_wo[i].astype(BF16))
        g_pre_f, g_post_f = row(norm_pre_ffn[i]), row(norm_post_ffn[i])
        x = _conv_ffn(x, mx, g_pre_f, g_post_f, fw, True)
        if not last:
            ctx = _conv_ffn(ctx_mixed, mc, g_pre_f, g_post_f, fw, False)
    return x
```

```python
import functools

import jax
import jax.numpy as jnp
from jax import lax
from jax.experimental import pallas as pl
from jax.experimental.pallas import tpu as pltpu

EPS = 1e-6
N_MIXERS = 2
GRID_W = 64
GLA_HEADS = 4
GATE_TAU = 16.0
CHUNK = 64
LOG2_E = 1.4426950408889634

V7X_LANES = 128
V7X_SUBLANES = 8
V7X_VMEM_BYTES = 64 * 1024 * 1024
VMEM_LIMIT = V7X_VMEM_BYTES * 7 // 8

F32 = jnp.float32
BF16 = jnp.bfloat16

SH1, SC1, G1, SH2, SC2, G2 = range(6)


def _params(*sem):
    return pltpu.CompilerParams(dimension_semantics=sem, vmem_limit_bytes=VMEM_LIMIT)


def _const_spec(shape):
    nd = len(shape)
    return pl.BlockSpec(shape, lambda *_: (0,) * nd, pipeline_mode=pl.Buffered(1))


def _rms(x, g):
    return x * lax.rsqrt(jnp.mean(x * x, axis=-1, keepdims=True) + EPS) * g


def _silu(x):
    return x * jax.nn.sigmoid(x)


def _norm_mod(x, g, mod_ref, shift_row, scale_row):
    h = _rms(x, g)
    return h * (1.0 + mod_ref[0, scale_row:scale_row + 1, :]) + mod_ref[0, shift_row:shift_row + 1, :]


def _residual(x, y, gpost, mod_ref, gate_row):
    return x + mod_ref[0, gate_row:gate_row + 1, :] * _rms(y, gpost)


def _dot(a, b):
    return jnp.dot(a, b, preferred_element_type=F32)


def _dot_nt(a, b):
    return lax.dot_general(a, b, (((1,), (1,)), ((), ())), preferred_element_type=F32)


def _dot_tn(a, b):
    return lax.dot_general(a, b, (((0,), (0,)), ((), ())), preferred_element_type=F32)


def _ada_kernel(cond_ref, w_ref, b_ref, o_ref):
    s = _silu(cond_ref[...])
    o_ref[0] = jnp.dot(s, w_ref[0], preferred_element_type=F32, precision=lax.Precision.HIGHEST) + b_ref[0]


def _ada_call(cond, ada_w, ada_b):
    depth, d, d6 = ada_w.shape
    rows = cond.shape[0]
    tn = d6 // 8
    return pl.pallas_call(
        _ada_kernel,
        grid=(depth, d6 // tn),
        in_specs=[
            pl.BlockSpec((rows, d), lambda i, j: (0, 0)),
            pl.BlockSpec((1, d, tn), lambda i, j: (i, 0, j)),
            pl.BlockSpec((1, 1, tn), lambda i, j: (i, 0, j)),
        ],
        out_specs=pl.BlockSpec((1, rows, tn), lambda i, j: (i, 0, j)),
        out_shape=jax.ShapeDtypeStruct((depth, rows, d6), F32),
        compiler_params=_params("parallel", "parallel"),
        name="ada",
    )(cond, ada_w, ada_b.reshape(depth, 1, d6))


def _cf1_kernel(x_ref, mod_ref, g_ref, w1_ref, b1_ref, o_ref):
    d = x_ref.shape[-1]
    hb = _norm_mod(x_ref[0], g_ref[...], mod_ref, SH1, SC1).astype(BF16)
    a = _dot(hb, w1_ref[:, :d]) + b1_ref[:, :d]
    gt = _dot(hb, w1_ref[:, d:]) + b1_ref[:, d:]
    o_ref[0] = a * jax.nn.sigmoid(gt)


def _cf1_call(x, mods, g, w1, b1, tn):
    bsz, n, d = x.shape
    return pl.pallas_call(
        _cf1_kernel,
        grid=(bsz, n // tn),
        in_specs=[
            pl.BlockSpec((1, tn, d), lambda b, t: (b, t, 0)),
            pl.BlockSpec((1, 6, d), lambda b, t: (b, 0, 0)),
            _const_spec((1, d)),
            _const_spec((d, 2 * d)),
            _const_spec((1, 2 * d)),
        ],
        out_specs=pl.BlockSpec((1, tn, d), lambda b, t: (b, t, 0)),
        out_shape=jax.ShapeDtypeStruct((bsz, n, d), F32),
        compiler_params=_params("parallel", "parallel"),
        name="cf1",
    )(x, mods, g, w1, b1)


CF_HALO = 16
CF_ROWS = 64


def _cf2_kernel(glu_ref, prev_ref, next_ref, dw_ref, dwb_ref, lng_ref, lnb_ref, w2_ref, b2_ref,
                x_ref, mod_ref, gpost_ref, o_ref, buf_ref, u_ref, *, width):
    t = pl.program_id(1)
    nt = pl.num_programs(1)
    tn = glu_ref.shape[1]
    pad = width // 2
    buf_ref[0:CF_HALO] = jnp.where(t > 0, prev_ref[0], 0.0)
    buf_ref[CF_HALO:CF_HALO + tn] = glu_ref[0]
    buf_ref[CF_HALO + tn:] = jnp.where(t < nt - 1, next_ref[0], 0.0)
    lead = CF_HALO - pad
    groups = -(-(lead + width) // V7X_SUBLANES)
    d = glu_ref.shape[2]
    for r0 in range(0, tn, CF_ROWS):
        for l0 in range(0, d, V7X_LANES):
            ls = slice(l0, l0 + V7X_LANES)
            acc = None
            for r in range(V7X_SUBLANES):
                aligned = None
                for a in range(groups):
                    k = V7X_SUBLANES * a + r - lead
                    if 0 <= k < width:
                        lo = r0 + V7X_SUBLANES * a
                        term = buf_ref[lo:lo + CF_ROWS + V7X_SUBLANES, ls] * dw_ref[k:k + 1, ls]
                        aligned = term if aligned is None else aligned + term
                piece = aligned[r:r + CF_ROWS]
                acc = piece if acc is None else acc + piece
            u_ref[r0:r0 + CF_ROWS, ls] = acc
    u = u_ref[...] + dwb_ref[...]
    mu = jnp.mean(u, axis=-1, keepdims=True)
    uc = u - mu
    var = jnp.mean(uc * uc, axis=-1, keepdims=True)
    v = _silu(uc * lax.rsqrt(var + EPS) * lng_ref[...] + lnb_ref[...])
    y = _dot(v.astype(BF16), w2_ref[...]) + b2_ref[...]
    o_ref[0] = _residual(x_ref[0], y, gpost_ref[...], mod_ref, G1)


def _cf2_call(glu, x, mods, dw, dwb, lng, lnb, w2, b2, gpost, tn):
    bsz, n, d = x.shape
    width = dw.shape[0]
    assert width // 2 < CF_HALO and tn % CF_ROWS == 0
    hb = tn // CF_HALO
    last = n // CF_HALO - 1
    return pl.pallas_call(
        functools.partial(_cf2_kernel, width=width),
        grid=(bsz, n // tn),
        in_specs=[
            pl.BlockSpec((1, tn, d), lambda b, t: (b, t, 0)),
            pl.BlockSpec((1, CF_HALO, d), lambda b, t: (b, jnp.maximum(t * hb - 1, 0), 0)),
            pl.BlockSpec((1, CF_HALO, d), lambda b, t: (b, jnp.minimum((t + 1) * hb, last), 0)),
            _const_spec((width, d)),
            _const_spec((1, d)),
            _const_spec((1, d)),
            _const_spec((1, d)),
            _const_spec((d, d)),
            _const_spec((1, d)),
            pl.BlockSpec((1, tn, d), lambda b, t: (b, t, 0)),
            pl.BlockSpec((1, 6, d), lambda b, t: (b, 0, 0)),
            _const_spec((1, d)),
        ],
        out_specs=pl.BlockSpec((1, tn, d), lambda b, t: (b, t, 0)),
        out_shape=jax.ShapeDtypeStruct((bsz, n, d), F32),
        scratch_shapes=[pltpu.VMEM((tn + 2 * CF_HALO, d), F32), pltpu.VMEM((tn, d), F32)],
        compiler_params=_params("parallel", "parallel"),
        name="cf2",
    )(glu, glu, glu, dw, dwb, lng, lnb, w2, b2, x, mods, gpost)


def _ffn1_kernel(x_ref, mod_ref, g_ref, wa_ref, a_ref):
    hb = _norm_mod(x_ref[0], g_ref[...], mod_ref, SH2, SC2).astype(BF16)
    a_ref[0] = _dot(hb, wa_ref[...])


def _ffn1_call(x, mods, g, wa, tn):
    bsz, n, d = x.shape
    f = wa.shape[1]
    return pl.pallas_call(
        _ffn1_kernel,
        grid=(bsz, n // tn),
        in_specs=[
            pl.BlockSpec((1, tn, d), lambda b, t: (b, t, 0)),
            pl.BlockSpec((1, 6, d), lambda b, t: (b, 0, 0)),
            _const_spec((1, d)),
            _const_spec((d, f)),
        ],
        out_specs=pl.BlockSpec((1, tn, f), lambda b, t: (b, t, 0)),
        out_shape=jax.ShapeDtypeStruct((bsz, n, f), F32),
        compiler_params=_params("parallel", "parallel"),
        name="ffn1",
    )(x, mods, g, wa)


FFN_LANES = 2 * V7X_LANES
FFN_GROUP = 4 * FFN_LANES


def _ffn2_kernel(*refs, gw, vertical):
    if vertical:
        a_ref, ap_ref, an_ref = refs[:3]
        refs = refs[3:]
    else:
        a_ref = refs[0]
        refs = refs[1:]
    dw_ref, dwb_ref, wb_ref, wo_ref, x_ref, mod_ref, gpre_ref, gpost_ref, o_ref, gated_ref = refs
    t = pl.program_id(1)
    nt = pl.num_programs(1)
    tn, f = a_ref.shape[1:]
    nrows = tn // gw
    x = x_ref[0]
    hb = _norm_mod(x, gpre_ref[...], mod_ref, SH2, SC2).astype(BF16)
    y = None
    for f0 in range(0, f, FFN_LANES):
        fl = min(FFN_LANES, f - f0)
        fs = slice(f0, f0 + fl)
        sub = lax.broadcasted_iota(jnp.int32, (V7X_SUBLANES, fl), 0)
        w = [dw_ref[k:k + 1, fs] for k in range(dw_ref.shape[0])]
        bias = dwb_ref[:, fs]
        gate = _dot(hb, wb_ref[:, fs])
        for i in range(nrows):
            rs = slice(i * gw, (i + 1) * gw)
            mid = a_ref[0, rs, fs]
            if vertical:
                if i > 0:
                    up = a_ref[0, (i - 1) * gw:i * gw, fs]
                else:
                    up = jnp.where(t > 0, ap_ref[0, :, fs], 0.0)
                if i < nrows - 1:
                    dn = a_ref[0, (i + 1) * gw:(i + 2) * gw, fs]
                else:
                    dn = jnp.where(t < nt - 1, an_ref[0, :, fs], 0.0)
                left, centre, right = [w[j] * up + w[3 + j] * mid + w[6 + j] * dn for j in range(3)]
            else:
                left, centre, right = [w[j] * mid for j in range(3)]
            from_left = pltpu.roll(left, 1, 0)
            from_left = jnp.concatenate([jnp.where(sub > 0, from_left[:V7X_SUBLANES], 0.0),
                                         from_left[V7X_SUBLANES:]], axis=0)
            from_right = pltpu.roll(right, gw - 1, 0)
            from_right = jnp.concatenate([from_right[:gw - V7X_SUBLANES],
                                          jnp.where(sub < V7X_SUBLANES - 1, from_right[gw - V7X_SUBLANES:], 0.0)],
                                         axis=0)
            conv = centre + bias + from_left + from_right
            gated_ref[rs, fs] = (_silu(conv) * gate[rs]).astype(BF16)
        f1 = f0 + fl
        if f1 % FFN_GROUP == 0 or f1 == f:
            gs = slice((f0 // FFN_GROUP) * FFN_GROUP, f1)
            part = _dot(gated_ref[:, gs], wo_ref[gs, :])
            y = part if y is None else y + part
    o_ref[0] = _residual(x, y, gpost_ref[...], mod_ref, G2)


def _ffn2_call(a, x, mods, dw, dwb, wb, wo, gpre, gpost, tn, gw, vertical):
    bsz, n, d = x.shape
    f = a.shape[-1]
    assert tn % gw == 0
    hb = tn // gw
    last = n // gw - 1
    tile_f = pl.BlockSpec((1, tn, f), lambda b, t: (b, t, 0))
    tile_d = pl.BlockSpec((1, tn, d), lambda b, t: (b, t, 0))
    in_specs = [tile_f]
    args = [a]
    if vertical:
        in_specs += [pl.BlockSpec((1, gw, f), lambda b, t: (b, jnp.maximum(t * hb - 1, 0), 0)),
                     pl.BlockSpec((1, gw, f), lambda b, t: (b, jnp.minimum((t + 1) * hb, last), 0))]
        args += [a, a]
    in_specs += [_const_spec(dw.shape), _const_spec((1, f)), _const_spec((d, f)), _const_spec((f, d)), tile_d,
                 pl.BlockSpec((1, 6, d), lambda b, t: (b, 0, 0)), _const_spec((1, d)), _const_spec((1, d))]
    args += [dw, dwb, wb, wo, x, mods, gpre, gpost]
    return pl.pallas_call(
        functools.partial(_ffn2_kernel, gw=gw, vertical=vertical),
        grid=(bsz, n // tn),
        in_specs=in_specs,
        out_specs=tile_d,
        out_shape=jax.ShapeDtypeStruct((bsz, n, d), F32),
        scratch_shapes=[pltpu.VMEM((tn, f), BF16)],
        compiler_params=_params("parallel", "parallel"),
        name="ffn2_grid" if vertical else "ffn2_seq",
    )(*args)


def _log_sigmoid(z):
    return jnp.minimum(z, 0.0) - jnp.log1p(jnp.exp(-jnp.abs(z)))


def _gla_proj_kernel(x_ref, mod_ref, g_ref, wq_ref, wk_ref, wv_ref, wr_ref, wg1_ref, wg2_ref, bg_ref,
                     q_ref, k_ref, v_ref, r_ref, cf_ref, cb_ref, *, q_scale):
    tn = x_ref.shape[1]
    dk = q_ref.shape[-1]
    hb = _norm_mod(x_ref[0], g_ref[...], mod_ref, SH1, SC1).astype(BF16)
    q_ref[0] = _dot(hb, wq_ref[...]) * q_scale
    k_ref[0] = _dot(hb, wk_ref[...])
    v_ref[0] = _dot(hb, wv_ref[...])
    r_ref[0] = _silu(_dot(hb, wr_ref[...]))
    low = _dot(hb, wg1_ref[...]).astype(BF16)
    g = _log_sigmoid(_dot(low, wg2_ref[...]) + bg_ref[...]) * (LOG2_E / GATE_TAU)
    g_hi = g.astype(BF16)
    rest = g - g_hi.astype(F32)
    g_mid = rest.astype(BF16)
    g_lo = (rest - g_mid.astype(F32)).astype(BF16)
    row = lax.broadcasted_iota(jnp.int32, (CHUNK, 3 * CHUNK), 0)
    colm = lax.broadcasted_iota(jnp.int32, (CHUNK, 3 * CHUNK), 1) & (CHUNK - 1)
    tri_f = jnp.where(colm <= row, 1.0, 0.0).astype(BF16)
    tri_b = jnp.where(colm >= row, 1.0, 0.0).astype(BF16)
    for c0 in range(0, tn, CHUNK):
        cs = slice(c0, c0 + CHUNK)
        pieces = jnp.concatenate([g_hi[cs], g_mid[cs], g_lo[cs]], axis=0)
        cf_ref[0, cs, :] = _dot(tri_f, pieces[:, :dk])
        cb_ref[0, cs, :] = _dot(tri_b, pieces[:, dk:])


def _gla_proj_call(x, mods, g, wq, wk, wv, wr, wg1, wg2, bg, tn):
    bsz, n, d = x.shape
    dk, dv = wq.shape[1], wv.shape[1]
    tile = lambda w: pl.BlockSpec((1, tn, w), lambda b, t: (b, t, 0))
    shp = lambda w: jax.ShapeDtypeStruct((bsz, n, w), F32)
    return pl.pallas_call(
        functools.partial(_gla_proj_kernel, q_scale=float((dk // GLA_HEADS) ** -0.5)),
        grid=(bsz, n // tn),
        in_specs=[tile(d), pl.BlockSpec((1, 6, d), lambda b, t: (b, 0, 0)), _const_spec((1, d)),
                  _const_spec(wq.shape), _const_spec(wk.shape), _const_spec(wv.shape), _const_spec(wr.shape),
                  _const_spec(wg1.shape), _const_spec(wg2.shape), _const_spec(bg.shape)],
        out_specs=[tile(dk), tile(dk), tile(dv), tile(dv), tile(dk), tile(dk)],
        out_shape=[shp(dk), shp(dk), shp(dv), shp(dv), shp(dk), shp(dk)],
        compiler_params=_params("parallel", "parallel"),
        name="gla_proj",
    )(x, mods, g, wq, wk, wv, wr, wg1, wg2, bg)


SCAN_UNROLL = 2
CHUNK_LEVELS = 6
assert 1 << CHUNK_LEVELS == CHUNK


def _level_reference(c_ref, base, level, reverse):
    half = 1 << level
    blk = 2 * half
    off = half if reverse else half - 1
    dk = c_ref.shape[-1]

    def row(r):
        return c_ref[0, pl.ds(base + (r + off), 1), :]

    if blk >= V7X_SUBLANES:
        return jnp.concatenate([jnp.broadcast_to(row(b0), (blk, dk)) for b0 in range(0, CHUNK, blk)], axis=0)
    sub = lax.broadcasted_iota(jnp.int32, (V7X_SUBLANES, dk), 0)
    groups = []
    for g0 in range(0, CHUNK, V7X_SUBLANES):
        r = jnp.broadcast_to(row(g0), (V7X_SUBLANES, dk))
        for b0 in range(blk, V7X_SUBLANES, blk):
            r = jnp.where(sub >= b0, jnp.broadcast_to(row(g0 + b0), (V7X_SUBLANES, dk)), r)
        groups.append(r)
    return jnp.concatenate(groups, axis=0)


def _gla_scan_kernel(q_ref, k_ref, v_ref, cf_ref, cb_ref, sf0_ref, sb0_ref, o_ref, sff_ref, sbf_ref,
                     stf_ref, stb_ref, incf_ref, incb_ref, *, nchunks):
    stf_ref[...] = sf0_ref[0, 0]
    stb_ref[...] = sb0_ref[0, 0]
    dk = q_ref.shape[-1]
    ti = lax.broadcasted_iota(jnp.int32, (CHUNK, CHUNK), 0)
    si = lax.broadcasted_iota(jnp.int32, (CHUNK, CHUNK), 1)
    x = ti ^ si
    lvl = sum([(x >= (1 << b)).astype(jnp.int32) for b in range(CHUNK_LEVELS)]) - 1
    row = lax.broadcasted_iota(jnp.int32, (CHUNK, dk), 0)
    below_split = [((row >> level) & 1) == 1 for level in range(CHUNK_LEVELS)]

    def chunk_scores(base):
        rows = pl.ds(base, CHUNK)
        q, k, cf, cb = q_ref[0, rows, :], k_ref[0, rows, :], cf_ref[0, rows, :], cb_ref[0, rows, :]
        kb = k.astype(BF16)
        scores = None
        for level in range(CHUNK_LEVELS):
            half = 1 << level
            if level == 0:
                xq = jnp.where(below_split[0], cf - pltpu.roll(cf, 1, 0), cb - pltpu.roll(cb, CHUNK - 1, 0))
                part = _dot_nt((q * jnp.exp2(xq)).astype(BF16), kb)
                scores = jnp.where(lvl == 0, part, 0.0)
                continue
            if half >= V7X_SUBLANES:
                xq, xk = [], []
                for b0 in range(0, CHUNK, 2 * half):
                    above, below = slice(b0, b0 + half), slice(b0 + half, b0 + 2 * half)
                    rf = cf_ref[0, pl.ds(base + (b0 + half - 1), 1), :]
                    rb = cb_ref[0, pl.ds(base + (b0 + half), 1), :]
                    xq += [cb[above] - rb, cf[below] - rf]
                    xk += [rf - cf[above], rb - cb[below]]
                xq, xk = jnp.concatenate(xq, axis=0), jnp.concatenate(xk, axis=0)
            else:
                df = cf - _level_reference(cf_ref, base, level, False)
                db = cb - _level_reference(cb_ref, base, level, True)
                xq = jnp.where(below_split[level], df, db)
                xk = -jnp.where(below_split[level], db, df)
            part = _dot_nt((q * jnp.exp2(xq)).astype(BF16), (k * jnp.exp2(xk)).astype(BF16))
            scores = jnp.where(lvl == level, part, scores)
        return scores.astype(BF16)

    def state_increment(base, c_ref, end_row):
        rows = pl.ds(base, CHUNK)
        c_end = c_ref[0, pl.ds(base + end_row, 1), :]
        kd = (k_ref[0, rows, :] * jnp.exp2(c_end - c_ref[0, rows, :])).astype(BF16)
        return _dot_tn(kd, v_ref[0, rows, :].astype(BF16))

    def carried(base, c_ref, end_row, st_ref, inc):
        rows = pl.ds(base, CHUNK)
        st = st_ref[...]
        o = _dot((q_ref[0, rows, :] * jnp.exp2(c_ref[0, rows, :])).astype(BF16), st.astype(BF16))
        c_end = c_ref[0, pl.ds(base + end_row, 1), :]
        decay = jnp.broadcast_to(jnp.exp2(c_end), (dk, dk)).T[:, 0:1]
        st_ref[...] = st * decay + inc
        return o

    def chunk_base(ci):
        return pl.multiple_of(ci * CHUNK, CHUNK)

    def produce(i, slot):
        fbase = chunk_base(jnp.minimum(i, nchunks - 1))
        bbase = chunk_base(jnp.maximum(nchunks - 1 - i, 0))
        incf_ref[slot] = state_increment(fbase, cf_ref, CHUNK - 1)
        incb_ref[slot] = state_increment(bbase, cb_ref, 0)
        return chunk_scores(fbase)

    def step(i, scores, first):
        slot = i % 2
        next_scores = produce(i + 1, 1 - slot)
        fbase, bbase = chunk_base(i), chunk_base(nchunks - 1 - i)
        frows, brows = pl.ds(fbase, CHUNK), pl.ds(bbase, CHUNK)
        q, k, v = q_ref[0, frows, :], k_ref[0, frows, :], v_ref[0, frows, :]
        diag = jnp.sum(q * k, axis=-1, keepdims=True)
        o_f = (_dot(scores, v.astype(BF16)) + (2.0 * diag) * v
               + carried(fbase, cf_ref, CHUNK - 1, stf_ref, incf_ref[slot]))
        o_b = carried(bbase, cb_ref, 0, stb_ref, incb_ref[slot])
        if first:
            o_ref[0, frows, :] = o_f
            o_ref[0, brows, :] = o_b
        else:
            o_ref[0, frows, :] += o_f
            o_ref[0, brows, :] += o_b
        return next_scores

    unroll = min(SCAN_UNROLL, nchunks // 2)
    scores = produce(0, 0)
    scores = lax.fori_loop(0, nchunks // 2, lambda i, s: step(i, s, True), scores, unroll=unroll)
    lax.fori_loop(nchunks // 2, nchunks, lambda i, s: step(i, s, False), scores, unroll=unroll)
    sff_ref[0, 0] = stf_ref[...]
    sbf_ref[0, 0] = stb_ref[...]


def _gla_scan_call(q, k, v, cf, cb, s_f, s_b):
    bsz, n, dk = q.shape
    dv = v.shape[-1]
    hk, hv = dk // GLA_HEADS, dv // GLA_HEADS
    nchunks = n // CHUNK
    assert nchunks % 2 == 0
    tile = lambda w: pl.BlockSpec((1, n, w), lambda b, h: (b, 0, h))
    state = pl.BlockSpec((1, 1, hk, hv), lambda b, h: (b, h, 0, 0))
    state_shape = jax.ShapeDtypeStruct((bsz, GLA_HEADS, hk, hv), F32)
    return pl.pallas_call(
        functools.partial(_gla_scan_kernel, nchunks=nchunks),
        grid=(bsz, GLA_HEADS),
        in_specs=[tile(hk), tile(hk), tile(hv), tile(hk), tile(hk), state, state],
        out_specs=[tile(hv), state, state],
        out_shape=[jax.ShapeDtypeStruct((bsz, n, dv), F32), state_shape, state_shape],
        scratch_shapes=[pltpu.VMEM((hk, hv), F32), pltpu.VMEM((hk, hv), F32),
                        pltpu.VMEM((2, hk, hv), F32), pltpu.VMEM((2, hk, hv), F32)],
        compiler_params=_params("parallel", "parallel"),
        name="gla_scan",
    )(q, k, v, cf, cb, s_f, s_b)


def _gla_out_kernel(o_in_ref, r_ref, ng_ref, wo_ref, x_ref, mod_ref, gpost_ref, o_ref, *, heads):
    o = o_in_ref[0]
    hv = o.shape[-1] // heads
    parts = []
    for h in range(heads):
        oh = o[:, h * hv:(h + 1) * hv]
        parts.append(oh * lax.rsqrt(jnp.mean(oh * oh, axis=-1, keepdims=True) + EPS))
    gated = jnp.concatenate(parts, axis=-1) * ng_ref[...] * r_ref[0]
    y = _dot(gated.astype(BF16), wo_ref[...])
    o_ref[0] = _residual(x_ref[0], y, gpost_ref[...], mod_ref, G1)


def _gla_out_call(o, r, ng, wo, x, mods, gpost, tn):
    bsz, n, d = x.shape
    dv = o.shape[-1]
    tile_v = pl.BlockSpec((1, tn, dv), lambda b, t: (b, t, 0))
    tile_d = pl.BlockSpec((1, tn, d), lambda b, t: (b, t, 0))
    return pl.pallas_call(
        functools.partial(_gla_out_kernel, heads=GLA_HEADS),
        grid=(bsz, n // tn),
        in_specs=[tile_v, tile_v, _const_spec((1, dv)), _const_spec((dv, d)), tile_d,
                  pl.BlockSpec((1, 6, d), lambda b, t: (b, 0, 0)), _const_spec((1, d))],
        out_specs=tile_d,
        out_shape=jax.ShapeDtypeStruct((bsz, n, d), F32),
        compiler_params=_params("parallel", "parallel"),
        name="gla_out",
    )(o, r, ng, wo, x, mods, gpost)


def _tile(n, want):
    return min(n, want)


def _conformer(x, mods, g_pre, g_post, w):
    tn = _tile(x.shape[1], 512)
    glu = _cf1_call(x, mods, g_pre, w["w1"], w["b1"], tn)
    return _cf2_call(glu, x, mods, w["dw"], w["dwb"], w["ln_g"], w["ln_b"], w["w2"], w["b2"], g_post, tn)


def _gla_states(h, mods, g_pre, w, s_f, s_b):
    n = h.shape[1]
    q, k, v, r, cf, cb = _gla_proj_call(h, mods, g_pre, w["wq"], w["wk"], w["wv"], w["wr"], w["wg1"], w["wg2"],
                                        w["bg"], _tile(n, 512))
    o, fin_f, fin_b = _gla_scan_call(q, k, v, cf, cb, s_f, s_b)
    return o, r, fin_f, fin_b


def _gla_readout(x, mods, g_post, w, o, r):
    return _gla_out_call(o, r, w["norm_g"], w["wo"], x, mods, g_post, _tile(x.shape[1], 512))


def _conv_ffn(x, mods, g_pre, g_post, w, on_grid):
    n = x.shape[1]
    tn = _tile(n, 256)
    a = _ffn1_call(x, mods, g_pre, w["wa"], _tile(n, 512))
    if on_grid:
        return _ffn2_call(a, x, mods, w["dw"], w["dwb"], w["wb"], w["wo"], g_pre, g_post, tn, GRID_W, True)
    return _ffn2_call(a, x, mods, w["dw"][3:6], w["dwb"], w["wb"], w["wo"], g_pre, g_post, n, n, False)


def kernel(x, c, ctx, c_ctx, ada_w, ada_b, norm_pre_mix, norm_post_mix, norm_pre_ffn, norm_post_ffn, cf_w1, cf_b1, cf_dw, cf_dwb, cf_ln_g, cf_ln_b, cf_w2, cf_b2, gla_wq, gla_wk, gla_wv, gla_wr, gla_wg1, gla_wg2, gla_bg, gla_norm_g, gla_wo, ffn_wa, ffn_wb, ffn_dw, ffn_dwb, ffn_wo):
    bsz, n, d = x.shape
    depth = ada_w.shape[0]
    dk = gla_wq.shape[-1]
    dv = gla_wv.shape[-1]
    hk, hv = dk // GLA_HEADS, dv // GLA_HEADS
    row = lambda v: v.reshape(1, -1)

    cond_rows = -(-(bsz + 1) // V7X_SUBLANES) * V7X_SUBLANES
    cond = jnp.zeros((cond_rows, d), F32).at[:bsz].set(c).at[bsz].set(c_ctx)
    mods = _ada_call(cond, ada_w, ada_b)

    for i in range(depth):
        last = i == depth - 1
        j = i // N_MIXERS
        mx = mods[i, :bsz].reshape(bsz, 6, d)
        mc = jnp.broadcast_to(mods[i, bsz].reshape(1, 6, d), (bsz, 6, d))
        g_pre, g_post = row(norm_pre_mix[i]), row(norm_post_mix[i])
        if i % N_MIXERS == 0:
            w = dict(w1=cf_w1[j].astype(BF16), b1=row(cf_b1[j]), dw=cf_dw[j], dwb=row(cf_dwb[j]),
                     ln_g=row(cf_ln_g[j]), ln_b=row(cf_ln_b[j]), w2=cf_w2[j].astype(BF16), b2=row(cf_b2[j]))
            x = _conformer(x, mx, g_pre, g_post, w)
            if not last:
                ctx_mixed = _conformer(ctx, mc, g_pre, g_post, w)
        else:
            rank = gla_wg1.shape[-1]
            wg2 = jnp.zeros((2 * rank, 2 * dk), F32)
            wg2 = wg2.at[:rank, :dk].set(gla_wg2[j, 0]).at[rank:, dk:].set(gla_wg2[j, 1])
            w = dict(wq=gla_wq[j].astype(BF16), wk=gla_wk[j].astype(BF16), wv=gla_wv[j].astype(BF16),
                     wr=gla_wr[j].astype(BF16),
                     wg1=jnp.concatenate([gla_wg1[j, 0], gla_wg1[j, 1]], axis=-1).astype(BF16),
                     wg2=wg2.astype(BF16), bg=gla_bg[j].reshape(1, 2 * dk),
                     norm_g=jnp.tile(gla_norm_g[j], GLA_HEADS).reshape(1, dv), wo=gla_wo[j].astype(BF16))
            zeros = jnp.zeros((bsz, GLA_HEADS, hk, hv), F32)
            oc, rc, s_f, s_b = _gla_states(ctx, mc, g_pre, w, zeros, zeros)
            if not last:
                ctx_mixed = _gla_readout(ctx, mc, g_post, w, oc, rc)
            ox, rx, _, _ = _gla_states(x, mx, g_pre, w, s_f, s_b)
            x = _gla_readout(x, mx, g_post, w, ox, rx)
        fw = dict(wa=ffn_wa[i].astype(BF16), wb=ffn_wb[i].astype(BF16), dw=ffn_dw[i].reshape(-1, ffn_dw.shape[-1]),
                  dwb=row(ffn_dwb[i]), wo=ffn_wo[i].astype(BF16))
        g_pre_f, g_post_f = row(norm_pre_ffn[i]), row(norm_post_ffn[i])
        x = _conv_ffn(x, mx, g_pre_f, g_post_f, fw, True)
        if not last:
            ctx = _conv_ffn(ctx_mixed, mc, g_pre_f, g_post_f, fw, False)
    return x
```

```python
import functools

import jax
import jax.numpy as jnp
from jax import lax
from jax.experimental import pallas as pl
from jax.experimental.pallas import tpu as pltpu

EPS = 1e-6
N_MIXERS = 2
GRID_W = 64
GLA_HEADS = 4
GATE_TAU = 16.0
CHUNK = 64
LOG2_E = 1.4426950408889634

V7X_LANES = 128
V7X_SUBLANES = 8
V7X_VMEM_BYTES = 64 * 1024 * 1024
VMEM_LIMIT = V7X_VMEM_BYTES * 7 // 8

F32 = jnp.float32
BF16 = jnp.bfloat16

SH1, SC1, G1, SH2, SC2, G2 = range(6)


def _params(*sem):
    return pltpu.CompilerParams(dimension_semantics=sem, vmem_limit_bytes=VMEM_LIMIT)


def _const_spec(shape):
    nd = len(shape)
    return pl.BlockSpec(shape, lambda *_: (0,) * nd, pipeline_mode=pl.Buffered(1))


def _rms(x, g):
    return x * lax.rsqrt(jnp.mean(x * x, axis=-1, keepdims=True) + EPS) * g


def _silu(x):
    return x * jax.nn.sigmoid(x)


def _norm_mod(x, g, mod_ref, shift_row, scale_row):
    h = _rms(x, g)
    return h * (1.0 + mod_ref[0, scale_row:scale_row + 1, :]) + mod_ref[0, shift_row:shift_row + 1, :]


def _residual(x, y, gpost, mod_ref, gate_row):
    return x + mod_ref[0, gate_row:gate_row + 1, :] * _rms(y, gpost)


def _dot(a, b):
    return jnp.dot(a, b, preferred_element_type=F32)


def _dot_nt(a, b):
    return lax.dot_general(a, b, (((1,), (1,)), ((), ())), preferred_element_type=F32)


def _dot_tn(a, b):
    return lax.dot_general(a, b, (((0,), (0,)), ((), ())), preferred_element_type=F32)


def _ada_kernel(cond_ref, w_ref, b_ref, o_ref):
    s = _silu(cond_ref[...])
    o_ref[0] = jnp.dot(s, w_ref[0], preferred_element_type=F32, precision=lax.Precision.HIGHEST) + b_ref[0]


def _ada_call(cond, ada_w, ada_b):
    depth, d, d6 = ada_w.shape
    rows = cond.shape[0]
    tn = d6 // 8
    return pl.pallas_call(
        _ada_kernel,
        grid=(depth, d6 // tn),
        in_specs=[
            pl.BlockSpec((rows, d), lambda i, j: (0, 0)),
            pl.BlockSpec((1, d, tn), lambda i, j: (i, 0, j)),
            pl.BlockSpec((1, 1, tn), lambda i, j: (i, 0, j)),
        ],
        out_specs=pl.BlockSpec((1, rows, tn), lambda i, j: (i, 0, j)),
        out_shape=jax.ShapeDtypeStruct((depth, rows, d6), F32),
        compiler_params=_params("parallel", "parallel"),
        name="ada",
    )(cond, ada_w, ada_b.reshape(depth, 1, d6))


def _cf1_kernel(x_ref, mod_ref, g_ref, w1_ref, b1_ref, o_ref):
    d = x_ref.shape[-1]
    hb = _norm_mod(x_ref[0], g_ref[...], mod_ref, SH1, SC1).astype(BF16)
    a = _dot(hb, w1_ref[:, :d]) + b1_ref[:, :d]
    gt = _dot(hb, w1_ref[:, d:]) + b1_ref[:, d:]
    o_ref[0] = a * jax.nn.sigmoid(gt)


def _cf1_call(x, mods, g, w1, b1, tn):
    bsz, n, d = x.shape
    return pl.pallas_call(
        _cf1_kernel,
        grid=(bsz, n // tn),
        in_specs=[
            pl.BlockSpec((1, tn, d), lambda b, t: (b, t, 0)),
            pl.BlockSpec((1, 6, d), lambda b, t: (b, 0, 0)),
            _const_spec((1, d)),
            _const_spec((d, 2 * d)),
            _const_spec((1, 2 * d)),
        ],
        out_specs=pl.BlockSpec((1, tn, d), lambda b, t: (b, t, 0)),
        out_shape=jax.ShapeDtypeStruct((bsz, n, d), F32),
        compiler_params=_params("parallel", "parallel"),
        name="cf1",
    )(x, mods, g, w1, b1)


CF_HALO = 16
CF_ROWS = 128


def _cf2_kernel(glu_ref, prev_ref, next_ref, dw_ref, dwb_ref, lng_ref, lnb_ref, w2_ref, b2_ref,
                x_ref, mod_ref, gpost_ref, o_ref, buf_ref, u_ref, *, width):
    t = pl.program_id(1)
    nt = pl.num_programs(1)
    tn = glu_ref.shape[1]
    pad = width // 2
    buf_ref[0:CF_HALO] = jnp.where(t > 0, prev_ref[0], 0.0)
    buf_ref[CF_HALO:CF_HALO + tn] = glu_ref[0]
    buf_ref[CF_HALO + tn:] = jnp.where(t < nt - 1, next_ref[0], 0.0)
    lead = CF_HALO - pad
    groups = -(-(lead + width) // V7X_SUBLANES)
    d = glu_ref.shape[2]
    for r0 in range(0, tn, CF_ROWS):
        for l0 in range(0, d, V7X_LANES):
            ls = slice(l0, l0 + V7X_LANES)
            acc = None
            for r in range(V7X_SUBLANES):
                aligned = None
                for a in range(groups):
                    k = V7X_SUBLANES * a + r - lead
                    if 0 <= k < width:
                        lo = r0 + V7X_SUBLANES * a
                        term = buf_ref[lo:lo + CF_ROWS + V7X_SUBLANES, ls] * dw_ref[k:k + 1, ls]
                        aligned = term if aligned is None else aligned + term
                piece = aligned[r:r + CF_ROWS]
                acc = piece if acc is None else acc + piece
            u_ref[r0:r0 + CF_ROWS, ls] = acc
    u = u_ref[...] + dwb_ref[...]
    mu = jnp.mean(u, axis=-1, keepdims=True)
    uc = u - mu
    var = jnp.mean(uc * uc, axis=-1, keepdims=True)
    v = _silu(uc * lax.rsqrt(var + EPS) * lng_ref[...] + lnb_ref[...])
    y = _dot(v.astype(BF16), w2_ref[...]) + b2_ref[...]
    o_ref[0] = _residual(x_ref[0], y, gpost_ref[...], mod_ref, G1)


def _cf2_call(glu, x, mods, dw, dwb, lng, lnb, w2, b2, gpost, tn):
    bsz, n, d = x.shape
    width = dw.shape[0]
    assert width // 2 < CF_HALO and tn % CF_ROWS == 0
    hb = tn // CF_HALO
    last = n // CF_HALO - 1
    return pl.pallas_call(
        functools.partial(_cf2_kernel, width=width),
        grid=(bsz, n // tn),
        in_specs=[
            pl.BlockSpec((1, tn, d), lambda b, t: (b, t, 0)),
            pl.BlockSpec((1, CF_HALO, d), lambda b, t: (b, jnp.maximum(t * hb - 1, 0), 0)),
            pl.BlockSpec((1, CF_HALO, d), lambda b, t: (b, jnp.minimum((t + 1) * hb, last), 0)),
            _const_spec((width, d)),
            _const_spec((1, d)),
            _const_spec((1, d)),
            _const_spec((1, d)),
            _const_spec((d, d)),
            _const_spec((1, d)),
            pl.BlockSpec((1, tn, d), lambda b, t: (b, t, 0)),
            pl.BlockSpec((1, 6, d), lambda b, t: (b, 0, 0)),
            _const_spec((1, d)),
        ],
        out_specs=pl.BlockSpec((1, tn, d), lambda b, t: (b, t, 0)),
        out_shape=jax.ShapeDtypeStruct((bsz, n, d), F32),
        scratch_shapes=[pltpu.VMEM((tn + 2 * CF_HALO, d), F32), pltpu.VMEM((tn, d), F32)],
        compiler_params=_params("parallel", "parallel"),
        name="cf2",
    )(glu, glu, glu, dw, dwb, lng, lnb, w2, b2, x, mods, gpost)


FFN_LANES = 2 * V7X_LANES
FFN_GROUP = 2 * FFN_LANES


def _ffn_kernel(*refs, gw, vertical):
    if vertical:
        x_ref, xp_ref, xn_ref = refs[:3]
        refs = refs[3:]
    else:
        x_ref = refs[0]
        refs = refs[1:]
    dw_ref, dwb_ref, wa_ref, wb_ref, wo_ref, mod_ref, gpre_ref, gpost_ref, o_ref, gated_ref = refs
    t = pl.program_id(1)
    nt = pl.num_programs(1)
    tn = x_ref.shape[1]
    f = wa_ref.shape[1]
    nrows = tn // gw
    x = x_ref[0]
    gpre = gpre_ref[...]
    hb = _norm_mod(x, gpre, mod_ref, SH2, SC2).astype(BF16)
    if vertical:
        before = jnp.where(t > 0, _norm_mod(xp_ref[0], gpre, mod_ref, SH2, SC2), 0.0).astype(BF16)
        after = jnp.where(t < nt - 1, _norm_mod(xn_ref[0], gpre, mod_ref, SH2, SC2), 0.0).astype(BF16)
        hb_ext = jnp.concatenate([before, hb, after], axis=0)
    else:
        hb_ext = hb
    y = None
    strips = [slice(f0, min(f0 + FFN_LANES, f)) for f0 in range(0, f, FFN_LANES)]
    groups = [slice(g0, min(g0 + FFN_GROUP, f)) for g0 in range(0, f, FFN_GROUP)]

    def project(fs):
        return _dot(hb_ext, wa_ref[:, fs]), _dot(hb, wb_ref[:, fs])

    projected = project(strips[0])
    for si, fs in enumerate(strips):
        fl = fs.stop - fs.start
        a_ext, gate = projected
        if si + 1 < len(strips):
            projected = project(strips[si + 1])
        if fs.start % FFN_GROUP == 0 and fs.start > 0:
            gs = groups[fs.start // FFN_GROUP - 1]
            part = _dot(gated_ref[:, gs], wo_ref[gs, :])
            y = part if y is None else y + part
        sub = lax.broadcasted_iota(jnp.int32, (V7X_SUBLANES, fl), 0)
        w = [dw_ref[k:k + 1, fs] for k in range(dw_ref.shape[0])]
        bias = dwb_ref[:, fs]
        for i in range(nrows):
            rs = slice(i * gw, (i + 1) * gw)
            if vertical:
                up, mid, dn = [a_ext[(i + j) * gw:(i + j + 1) * gw] for j in range(3)]
                left, centre, right = [w[j] * up + w[3 + j] * mid + w[6 + j] * dn for j in range(3)]
            else:
                mid = a_ext[rs]
                left, centre, right = [w[j] * mid for j in range(3)]
            from_left = pltpu.roll(left, 1, 0)
            from_left = jnp.concatenate([jnp.where(sub > 0, from_left[:V7X_SUBLANES], 0.0),
                                         from_left[V7X_SUBLANES:]], axis=0)
            from_right = pltpu.roll(right, gw - 1, 0)
            from_right = jnp.concatenate([from_right[:gw - V7X_SUBLANES],
                                          jnp.where(sub < V7X_SUBLANES - 1, from_right[gw - V7X_SUBLANES:], 0.0)],
                                         axis=0)
            conv = centre + bias + from_left + from_right
            gated_ref[rs, fs] = (_silu(conv) * gate[rs]).astype(BF16)
    part = _dot(gated_ref[:, groups[-1]], wo_ref[groups[-1], :])
    y = part if y is None else y + part
    o_ref[0] = _residual(x, y, gpost_ref[...], mod_ref, G2)


def _ffn_call(x, mods, dw, dwb, wa, wb, wo, gpre, gpost, tn, gw, vertical):
    bsz, n, d = x.shape
    f = wa.shape[-1]
    assert tn % gw == 0
    hb = tn // gw
    last = n // gw - 1
    tile_d = pl.BlockSpec((1, tn, d), lambda b, t: (b, t, 0))
    in_specs = [tile_d]
    args = [x]
    if vertical:
        in_specs += [pl.BlockSpec((1, gw, d), lambda b, t: (b, jnp.maximum(t * hb - 1, 0), 0)),
                     pl.BlockSpec((1, gw, d), lambda b, t: (b, jnp.minimum((t + 1) * hb, last), 0))]
        args += [x, x]
    in_specs += [_const_spec(dw.shape), _const_spec((1, f)), _const_spec((d, f)), _const_spec((d, f)),
                 _const_spec((f, d)), pl.BlockSpec((1, 6, d), lambda b, t: (b, 0, 0)), _const_spec((1, d)),
                 _const_spec((1, d))]
    args += [dw, dwb, wa, wb, wo, mods, gpre, gpost]
    return pl.pallas_call(
        functools.partial(_ffn_kernel, gw=gw, vertical=vertical),
        grid=(bsz, n // tn),
        in_specs=in_specs,
        out_specs=tile_d,
        out_shape=jax.ShapeDtypeStruct((bsz, n, d), F32),
        scratch_shapes=[pltpu.VMEM((tn, f), BF16)],
        compiler_params=_params("parallel", "parallel"),
        name="ffn_grid" if vertical else "ffn_seq",
    )(*args)


def _log_sigmoid(z):
    return jnp.minimum(z, 0.0) - jnp.log(1.0 + jnp.exp(-jnp.abs(z)))


def _gla_proj_kernel(x_ref, mod_ref, g_ref, wq_ref, wk_ref, wv_ref, wr_ref, wg1_ref, wg2_ref, bg_ref,
                     q_ref, k_ref, v_ref, r_ref, cf_ref, cb_ref, *, q_scale):
    tn = x_ref.shape[1]
    dk = q_ref.shape[-1]
    hb = _norm_mod(x_ref[0], g_ref[...], mod_ref, SH1, SC1).astype(BF16)
    low = _dot(hb, wg1_ref[...]).astype(BF16)
    z = _dot(low, wg2_ref[...]) + bg_ref[...]
    q_ref[0] = _dot(hb, wq_ref[...]) * q_scale
    k_ref[0] = _dot(hb, wk_ref[...])
    g = _log_sigmoid(z) * (LOG2_E / GATE_TAU)
    v_ref[0] = _dot(hb, wv_ref[...])
    r_ref[0] = _silu(_dot(hb, wr_ref[...]))
    g_hi = g.astype(BF16)
    rest = g - g_hi.astype(F32)
    g_mid = rest.astype(BF16)
    g_lo = (rest - g_mid.astype(F32)).astype(BF16)
    row = lax.broadcasted_iota(jnp.int32, (CHUNK, 3 * CHUNK), 0)
    colm = lax.broadcasted_iota(jnp.int32, (CHUNK, 3 * CHUNK), 1) & (CHUNK - 1)
    tri_f = jnp.where(colm <= row, 1.0, 0.0).astype(BF16)
    tri_b = jnp.where(colm >= row, 1.0, 0.0).astype(BF16)
    for c0 in range(0, tn, CHUNK):
        cs = slice(c0, c0 + CHUNK)
        pieces = jnp.concatenate([g_hi[cs], g_mid[cs], g_lo[cs]], axis=0)
        cf_ref[0, cs, :] = _dot(tri_f, pieces[:, :dk])
        cb_ref[0, cs, :] = _dot(tri_b, pieces[:, dk:])


def _gla_proj_call(x, mods, g, wq, wk, wv, wr, wg1, wg2, bg, tn):
    bsz, n, d = x.shape
    dk, dv = wq.shape[1], wv.shape[1]
    tile = lambda w: pl.BlockSpec((1, tn, w), lambda b, t: (b, t, 0))
    shp = lambda w: jax.ShapeDtypeStruct((bsz, n, w), F32)
    return pl.pallas_call(
        functools.partial(_gla_proj_kernel, q_scale=float((dk // GLA_HEADS) ** -0.5)),
        grid=(bsz, n // tn),
        in_specs=[tile(d), pl.BlockSpec((1, 6, d), lambda b, t: (b, 0, 0)), _const_spec((1, d)),
                  _const_spec(wq.shape), _const_spec(wk.shape), _const_spec(wv.shape), _const_spec(wr.shape),
                  _const_spec(wg1.shape), _const_spec(wg2.shape), _const_spec(bg.shape)],
        out_specs=[tile(dk), tile(dk), tile(dv), tile(dv), tile(dk), tile(dk)],
        out_shape=[shp(dk), shp(dk), shp(dv), shp(dv), shp(dk), shp(dk)],
        compiler_params=_params("parallel", "parallel"),
        name="gla_proj",
    )(x, mods, g, wq, wk, wv, wr, wg1, wg2, bg)


CHUNK_LEVELS = 6
assert 1 << CHUNK_LEVELS == CHUNK


def _level_reference(c_ref, base, level, reverse):
    half = 1 << level
    blk = 2 * half
    off = half if reverse else half - 1
    dk = c_ref.shape[-1]

    def row(r):
        return c_ref[0, pl.ds(base + (r + off), 1), :]

    if blk >= V7X_SUBLANES:
        return jnp.concatenate([jnp.broadcast_to(row(b0), (blk, dk)) for b0 in range(0, CHUNK, blk)], axis=0)
    sub = lax.broadcasted_iota(jnp.int32, (V7X_SUBLANES, dk), 0)
    groups = []
    for g0 in range(0, CHUNK, V7X_SUBLANES):
        r = jnp.broadcast_to(row(g0), (V7X_SUBLANES, dk))
        for b0 in range(blk, V7X_SUBLANES, blk):
            r = jnp.where(sub >= b0, jnp.broadcast_to(row(g0 + b0), (V7X_SUBLANES, dk)), r)
        groups.append(r)
    return jnp.concatenate(groups, axis=0)


def _gla_scan_kernel(q_ref, k_ref, v_ref, cf_ref, cb_ref, sf0_ref, sb0_ref, o_ref, sff_ref, sbf_ref,
                     stf_ref, stb_ref, incf_ref, incb_ref, *, nchunks):
    stf_ref[...] = sf0_ref[0, 0]
    stb_ref[...] = sb0_ref[0, 0]
    dk = q_ref.shape[-1]
    ti = lax.broadcasted_iota(jnp.int32, (CHUNK, CHUNK), 0)
    si = lax.broadcasted_iota(jnp.int32, (CHUNK, CHUNK), 1)
    x = ti ^ si
    lvl = sum([(x >= (1 << b)).astype(jnp.int32) for b in range(CHUNK_LEVELS)]) - 1
    row = lax.broadcasted_iota(jnp.int32, (CHUNK, dk), 0)
    below_split = [((row >> level) & 1) == 1 for level in range(CHUNK_LEVELS)]

    def chunk_scores(base):
        rows = pl.ds(base, CHUNK)
        q, k, cf, cb = q_ref[0, rows, :], k_ref[0, rows, :], cf_ref[0, rows, :], cb_ref[0, rows, :]
        kb = k.astype(BF16)
        scores = None
        for level in range(CHUNK_LEVELS):
            half = 1 << level
            if level == 0:
                xq = jnp.where(below_split[0], cf - pltpu.roll(cf, 1, 0), cb - pltpu.roll(cb, CHUNK - 1, 0))
                part = _dot_nt((q * jnp.exp2(xq)).astype(BF16), kb)
                scores = jnp.where(lvl == 0, part, 0.0)
                continue
            if half >= V7X_SUBLANES:
                xq, xk = [], []
                for b0 in range(0, CHUNK, 2 * half):
                    above, below = slice(b0, b0 + half), slice(b0 + half, b0 + 2 * half)
                    rf = cf_ref[0, pl.ds(base + (b0 + half - 1), 1), :]
                    rb = cb_ref[0, pl.ds(base + (b0 + half), 1), :]
                    xq += [cb[above] - rb, cf[below] - rf]
                    xk += [rf - cf[above], rb - cb[below]]
                xq, xk = jnp.concatenate(xq, axis=0), jnp.concatenate(xk, axis=0)
            else:
                df = cf - _level_reference(cf_ref, base, level, False)
                db = cb - _level_reference(cb_ref, base, level, True)
                xq = jnp.where(below_split[level], df, db)
                xk = -jnp.where(below_split[level], db, df)
            part = _dot_nt((q * jnp.exp2(xq)).astype(BF16), (k * jnp.exp2(xk)).astype(BF16))
            scores = jnp.where(lvl == level, part, scores)
        return scores.astype(BF16)

    def state_increment(base, c_ref, end_row):
        rows = pl.ds(base, CHUNK)
        c_end = c_ref[0, pl.ds(base + end_row, 1), :]
        kd = (k_ref[0, rows, :] * jnp.exp2(c_end - c_ref[0, rows, :])).astype(BF16)
        return _dot_tn(kd, v_ref[0, rows, :].astype(BF16))

    def carried(base, c_ref, end_row, st_ref, inc):
        rows = pl.ds(base, CHUNK)
        st = st_ref[...]
        o = _dot((q_ref[0, rows, :] * jnp.exp2(c_ref[0, rows, :])).astype(BF16), st.astype(BF16))
        c_end = c_ref[0, pl.ds(base + end_row, 1), :]
        decay = jnp.broadcast_to(jnp.exp2(c_end), (dk, dk)).T[:, 0:1]
        st_ref[...] = st * decay + inc
        return o

    def chunk_base(ci):
        return pl.multiple_of(ci * CHUNK, CHUNK)

    def produce(i, slot):
        fbase = chunk_base(jnp.minimum(i, nchunks - 1))
        bbase = chunk_base(jnp.maximum(nchunks - 1 - i, 0))
        incf_ref[slot] = state_increment(fbase, cf_ref, CHUNK - 1)
        incb_ref[slot] = state_increment(bbase, cb_ref, 0)
        return chunk_scores(fbase)

    def step(i, scores, first, slot):
        next_scores = produce(i + 1, 1 - slot)
        fbase, bbase = chunk_base(i), chunk_base(nchunks - 1 - i)
        frows, brows = pl.ds(fbase, CHUNK), pl.ds(bbase, CHUNK)
        q, k, v = q_ref[0, frows, :], k_ref[0, frows, :], v_ref[0, frows, :]
        diag = jnp.sum(q * k, axis=-1, keepdims=True)
        o_f = (_dot(scores, v.astype(BF16)) + (2.0 * diag) * v
               + carried(fbase, cf_ref, CHUNK - 1, stf_ref, incf_ref[slot]))
        o_b = carried(bbase, cb_ref, 0, stb_ref, incb_ref[slot])
        if first:
            o_ref[0, frows, :] = o_f
            o_ref[0, brows, :] = o_b
        else:
            o_ref[0, frows, :] += o_f
            o_ref[0, brows, :] += o_b
        return next_scores

    half = nchunks // 2
    per_trip = 4 if half % 4 == 0 else 2

    def trip(first):
        def body(j, s):
            for u in range(per_trip):
                s = step(per_trip * j + u, s, first, u % 2)
            return s
        return body

    scores = produce(0, 0)
    scores = lax.fori_loop(0, half // per_trip, trip(True), scores)
    lax.fori_loop(half // per_trip, nchunks // per_trip, trip(False), scores)
    sff_ref[0, 0] = stf_ref[...]
    sbf_ref[0, 0] = stb_ref[...]


def _gla_scan_call(q, k, v, cf, cb, s_f, s_b):
    bsz, n, dk = q.shape
    dv = v.shape[-1]
    hk, hv = dk // GLA_HEADS, dv // GLA_HEADS
    nchunks = n // CHUNK
    assert nchunks % 4 == 0
    tile = lambda w: pl.BlockSpec((1, n, w), lambda b, h: (b, 0, h))
    state = pl.BlockSpec((1, 1, hk, hv), lambda b, h: (b, h, 0, 0))
    state_shape = jax.ShapeDtypeStruct((bsz, GLA_HEADS, hk, hv), F32)
    return pl.pallas_call(
        functools.partial(_gla_scan_kernel, nchunks=nchunks),
        grid=(bsz, GLA_HEADS),
        in_specs=[tile(hk), tile(hk), tile(hv), tile(hk), tile(hk), state, state],
        out_specs=[tile(hv), state, state],
        out_shape=[jax.ShapeDtypeStruct((bsz, n, dv), F32), state_shape, state_shape],
        scratch_shapes=[pltpu.VMEM((hk, hv), F32), pltpu.VMEM((hk, hv), F32),
                        pltpu.VMEM((2, hk, hv), F32), pltpu.VMEM((2, hk, hv), F32)],
        compiler_params=_params("parallel", "parallel"),
        name="gla_scan",
    )(q, k, v, cf, cb, s_f, s_b)


def _gla_out_kernel(o_in_ref, r_ref, ng_ref, wo_ref, x_ref, mod_ref, gpost_ref, o_ref, *, heads):
    o = o_in_ref[0]
    hv = o.shape[-1] // heads
    parts = []
    for h in range(heads):
        oh = o[:, h * hv:(h + 1) * hv]
        parts.append(oh * lax.rsqrt(jnp.mean(oh * oh, axis=-1, keepdims=True) + EPS))
    gated = jnp.concatenate(parts, axis=-1) * ng_ref[...] * r_ref[0]
    y = _dot(gated.astype(BF16), wo_ref[...])
    o_ref[0] = _residual(x_ref[0], y, gpost_ref[...], mod_ref, G1)


def _gla_out_call(o, r, ng, wo, x, mods, gpost, tn):
    bsz, n, d = x.shape
    dv = o.shape[-1]
    tile_v = pl.BlockSpec((1, tn, dv), lambda b, t: (b, t, 0))
    tile_d = pl.BlockSpec((1, tn, d), lambda b, t: (b, t, 0))
    return pl.pallas_call(
        functools.partial(_gla_out_kernel, heads=GLA_HEADS),
        grid=(bsz, n // tn),
        in_specs=[tile_v, tile_v, _const_spec((1, dv)), _const_spec((dv, d)), tile_d,
                  pl.BlockSpec((1, 6, d), lambda b, t: (b, 0, 0)), _const_spec((1, d))],
        out_specs=tile_d,
        out_shape=jax.ShapeDtypeStruct((bsz, n, d), F32),
        compiler_params=_params("parallel", "parallel"),
        name="gla_out",
    )(o, r, ng, wo, x, mods, gpost)


def _tile(n, want):
    return min(n, want)


def _conformer(x, mods, g_pre, g_post, w):
    tn = _tile(x.shape[1], 512)
    glu = _cf1_call(x, mods, g_pre, w["w1"], w["b1"], tn)
    return _cf2_call(glu, x, mods, w["dw"], w["dwb"], w["ln_g"], w["ln_b"], w["w2"], w["b2"], g_post, tn)


def _gla_states(h, mods, g_pre, w, s_f, s_b):
    n = h.shape[1]
    q, k, v, r, cf, cb = _gla_proj_call(h, mods, g_pre, w["wq"], w["wk"], w["wv"], w["wr"], w["wg1"], w["wg2"],
                                        w["bg"], _tile(n, 512))
    o, fin_f, fin_b = _gla_scan_call(q, k, v, cf, cb, s_f, s_b)
    return o, r, fin_f, fin_b


def _gla_readout(x, mods, g_post, w, o, r):
    return _gla_out_call(o, r, w["norm_g"], w["wo"], x, mods, g_post, _tile(x.shape[1], 512))


def _conv_ffn(x, mods, g_pre, g_post, w, on_grid):
    n = x.shape[1]
    if on_grid:
        return _ffn_call(x, mods, w["dw"], w["dwb"], w["wa"], w["wb"], w["wo"], g_pre, g_post, _tile(n, 512),
                         GRID_W, True)
    return _ffn_call(x, mods, w["dw"][3:6], w["dwb"], w["wa"], w["wb"], w["wo"], g_pre, g_post, n, n, False)


def kernel(x, c, ctx, c_ctx, ada_w, ada_b, norm_pre_mix, norm_post_mix, norm_pre_ffn, norm_post_ffn, cf_w1, cf_b1, cf_dw, cf_dwb, cf_ln_g, cf_ln_b, cf_w2, cf_b2, gla_wq, gla_wk, gla_wv, gla_wr, gla_wg1, gla_wg2, gla_bg, gla_norm_g, gla_wo, ffn_wa, ffn_wb, ffn_dw, ffn_dwb, ffn_wo):
    bsz, n, d = x.shape
    depth = ada_w.shape[0]
    dk = gla_wq.shape[-1]
    dv = gla_wv.shape[-1]
    hk, hv = dk // GLA_HEADS, dv // GLA_HEADS
    row = lambda v: v.reshape(1, -1)

    cond_rows = -(-(bsz + 1) // V7X_SUBLANES) * V7X_SUBLANES
    cond = jnp.zeros((cond_rows, d), F32).at[:bsz].set(c).at[bsz].set(c_ctx)
    mods = _ada_call(cond, ada_w, ada_b)

    for i in range(depth):
        last = i == depth - 1
        j = i // N_MIXERS
        mx = mods[i, :bsz].reshape(bsz, 6, d)
        mc = jnp.broadcast_to(mods[i, bsz].reshape(1, 6, d), (bsz, 6, d))
        g_pre, g_post = row(norm_pre_mix[i]), row(norm_post_mix[i])
        if i % N_MIXERS == 0:
            w = dict(w1=cf_w1[j].astype(BF16), b1=row(cf_b1[j]), dw=cf_dw[j], dwb=row(cf_dwb[j]),
                     ln_g=row(cf_ln_g[j]), ln_b=row(cf_ln_b[j]), w2=cf_w2[j].astype(BF16), b2=row(cf_b2[j]))
            x = _conformer(x, mx, g_pre, g_post, w)
            if not last:
                ctx_mixed = _conformer(ctx, mc, g_pre, g_post, w)
        else:
            rank = gla_wg1.shape[-1]
            wg2 = jnp.zeros((2 * rank, 2 * dk), F32)
            wg2 = wg2.at[:rank, :dk].set(gla_wg2[j, 0]).at[rank:, dk:].set(gla_wg2[j, 1])
            w = dict(wq=gla_wq[j].astype(BF16), wk=gla_wk[j].astype(BF16), wv=gla_wv[j].astype(BF16),
                     wr=gla_wr[j].astype(BF16),
                     wg1=jnp.concatenate([gla_wg1[j, 0], gla_wg1[j, 1]], axis=-1).astype(BF16),
                     wg2=wg2.astype(BF16), bg=gla_bg[j].reshape(1, 2 * dk),
                     norm_g=jnp.tile(gla_norm_g[j], GLA_HEADS).reshape(1, dv), wo=gla_wo[j].astype(BF16))
            zeros = jnp.zeros((bsz, GLA_HEADS, hk, hv), F32)
            oc, rc, s_f, s_b = _gla_states(ctx, mc, g_pre, w, zeros, zeros)
            if not last:
                ctx_mixed = _gla_readout(ctx, mc, g_post, w, oc, rc)
            ox, rx, _, _ = _gla_states(x, mx, g_pre, w, s_f, s_b)
            x = _gla_readout(x, mx, g_post, w, ox, rx)
        fw = dict(wa=ffn_wa[i].astype(BF16), wb=ffn_wb[i].astype(BF16), dw=ffn_dw[i].reshape(-1, ffn_dw.shape[-1]),
                  dwb=row(ffn_dwb[i]), wo=ffn_wo[i].astype(BF16))
        g_pre_f, g_post_f = row(norm_pre_ffn[i]), row(norm_post_ffn[i])
        x = _conv_ffn(x, mx, g_pre_f, g_post_f, fw, True)
        if not last:
            ctx = _conv_ffn(ctx_mixed, mc, g_pre_f, g_post_f, fw, False)
    return x
```

```python
import functools

import jax
import jax.numpy as jnp
from jax import lax
from jax.experimental import pallas as pl
from jax.experimental.pallas import tpu as pltpu

EPS = 1e-6
N_MIXERS = 2
GRID_W = 64
GLA_HEADS = 4
GATE_TAU = 16.0
CHUNK = 64
LOG2_E = 1.4426950408889634

V7X_LANES = 128
V7X_SUBLANES = 8
V7X_VMEM_BYTES = 64 * 1024 * 1024
VMEM_LIMIT = V7X_VMEM_BYTES * 7 // 8

F32 = jnp.float32
BF16 = jnp.bfloat16

SH1, SC1, G1, SH2, SC2, G2 = range(6)


def _params(*sem):
    return pltpu.CompilerParams(dimension_semantics=sem, vmem_limit_bytes=VMEM_LIMIT)


def _const_spec(shape):
    nd = len(shape)
    return pl.BlockSpec(shape, lambda *_: (0,) * nd, pipeline_mode=pl.Buffered(1))


def _rms(x, g):
    return x * lax.rsqrt(jnp.mean(x * x, axis=-1, keepdims=True) + EPS) * g


def _silu(x):
    return x * jax.nn.sigmoid(x)


def _norm_mod(x, g, mod_ref, shift_row, scale_row):
    h = _rms(x, g)
    return h * (1.0 + mod_ref[0, scale_row:scale_row + 1, :]) + mod_ref[0, shift_row:shift_row + 1, :]


def _residual(x, y, gpost, mod_ref, gate_row):
    return x + mod_ref[0, gate_row:gate_row + 1, :] * _rms(y, gpost)


def _dot(a, b):
    return jnp.dot(a, b, preferred_element_type=F32)


def _dot_nt(a, b):
    return lax.dot_general(a, b, (((1,), (1,)), ((), ())), preferred_element_type=F32)


def _dot_tn(a, b):
    return lax.dot_general(a, b, (((0,), (0,)), ((), ())), preferred_element_type=F32)


def _ada_kernel(cond_ref, w_ref, b_ref, o_ref):
    s = _silu(cond_ref[...])
    o_ref[0] = jnp.dot(s, w_ref[0], preferred_element_type=F32, precision=lax.Precision.HIGHEST) + b_ref[0]


def _ada_call(cond, ada_w, ada_b):
    depth, d, d6 = ada_w.shape
    rows = cond.shape[0]
    tn = d6 // 8
    return pl.pallas_call(
        _ada_kernel,
        grid=(depth, d6 // tn),
        in_specs=[
            pl.BlockSpec((rows, d), lambda i, j: (0, 0)),
            pl.BlockSpec((1, d, tn), lambda i, j: (i, 0, j)),
            pl.BlockSpec((1, 1, tn), lambda i, j: (i, 0, j)),
        ],
        out_specs=pl.BlockSpec((1, rows, tn), lambda i, j: (i, 0, j)),
        out_shape=jax.ShapeDtypeStruct((depth, rows, d6), F32),
        compiler_params=_params("parallel", "parallel"),
        name="ada",
    )(cond, ada_w, ada_b.reshape(depth, 1, d6))


CF_HALO = 16
CF_ROWS = 128
CF_LANES = 2 * V7X_LANES


def _cf_kernel(x_ref, xp_ref, xn_ref, mod_ref, gpre_ref, gpost_ref, w1_ref, b1_ref, dw_ref, dwb_ref, lng_ref,
               lnb_ref, w2_ref, b2_ref, o_ref, buf_ref, u_ref, *, width):
    t = pl.program_id(1)
    nt = pl.num_programs(1)
    tn, d = x_ref.shape[1:]
    pad = width // 2
    x = x_ref[0]
    gpre = gpre_ref[...]
    hb_ext = jnp.concatenate([_norm_mod(v, gpre, mod_ref, SH1, SC1).astype(BF16)
                              for v in (xp_ref[0], x, xn_ref[0])], axis=0)

    def stage_glu(cs):
        gate_cs = slice(d + cs.start, d + cs.stop)
        a = _dot(hb_ext, w1_ref[:, cs]) + b1_ref[:, cs]
        gt = _dot(hb_ext, w1_ref[:, gate_cs]) + b1_ref[:, gate_cs]
        glu = a * jax.nn.sigmoid(gt)
        buf_ref[0:CF_HALO, cs] = jnp.where(t > 0, glu[0:CF_HALO], 0.0)
        buf_ref[CF_HALO:CF_HALO + tn, cs] = glu[CF_HALO:CF_HALO + tn]
        buf_ref[CF_HALO + tn:, cs] = jnp.where(t < nt - 1, glu[CF_HALO + tn:], 0.0)

    lead = CF_HALO - pad
    groups = -(-(lead + width) // V7X_SUBLANES)
    strips = [slice(c0, c0 + CF_LANES) for c0 in range(0, d, CF_LANES)]
    stage_glu(strips[0])
    for si, cs in enumerate(strips):
        if si + 1 < len(strips):
            stage_glu(strips[si + 1])
        for l0 in range(cs.start, cs.stop, V7X_LANES):
            ls = slice(l0, l0 + V7X_LANES)
            for r0 in range(0, tn, CF_ROWS):
                acc = None
                for r in range(V7X_SUBLANES):
                    aligned = None
                    for a in range(groups):
                        k = V7X_SUBLANES * a + r - lead
                        if 0 <= k < width:
                            lo = r0 + V7X_SUBLANES * a
                            term = buf_ref[lo:lo + CF_ROWS + V7X_SUBLANES, ls] * dw_ref[k:k + 1, ls]
                            aligned = term if aligned is None else aligned + term
                    piece = aligned[r:r + CF_ROWS]
                    acc = piece if acc is None else acc + piece
                u_ref[r0:r0 + CF_ROWS, ls] = acc
    u = u_ref[...] + dwb_ref[...]
    mu = jnp.mean(u, axis=-1, keepdims=True)
    uc = u - mu
    var = jnp.mean(uc * uc, axis=-1, keepdims=True)
    v = _silu(uc * lax.rsqrt(var + EPS) * lng_ref[...] + lnb_ref[...])
    y = _dot(v.astype(BF16), w2_ref[...]) + b2_ref[...]
    o_ref[0] = _residual(x, y, gpost_ref[...], mod_ref, G1)


def _cf_call(x, mods, gpre, gpost, w1, b1, dw, dwb, lng, lnb, w2, b2, tn):
    bsz, n, d = x.shape
    width = dw.shape[0]
    assert width // 2 < CF_HALO and tn % CF_ROWS == 0 and d % CF_LANES == 0
    hb = tn // CF_HALO
    last = n // CF_HALO - 1
    tile = pl.BlockSpec((1, tn, d), lambda b, t: (b, t, 0))
    row = _const_spec((1, d))
    return pl.pallas_call(
        functools.partial(_cf_kernel, width=width),
        grid=(bsz, n // tn),
        in_specs=[
            tile,
            pl.BlockSpec((1, CF_HALO, d), lambda b, t: (b, jnp.maximum(t * hb - 1, 0), 0)),
            pl.BlockSpec((1, CF_HALO, d), lambda b, t: (b, jnp.minimum((t + 1) * hb, last), 0)),
            pl.BlockSpec((1, 6, d), lambda b, t: (b, 0, 0)),
            row, row, _const_spec((d, 2 * d)), _const_spec((1, 2 * d)), _const_spec((width, d)), row, row, row,
            _const_spec((d, d)), row,
        ],
        out_specs=tile,
        out_shape=jax.ShapeDtypeStruct((bsz, n, d), F32),
        scratch_shapes=[pltpu.VMEM((tn + 2 * CF_HALO, d), F32), pltpu.VMEM((tn, d), F32)],
        compiler_params=_params("parallel", "parallel"),
        name="cf",
    )(x, x, x, mods, gpre, gpost, w1, b1, dw, dwb, lng, lnb, w2, b2)


FFN_LANES = 2 * V7X_LANES
FFN_GROUP = 2 * FFN_LANES


def _ffn_kernel(*refs, gw, vertical):
    if vertical:
        x_ref, xp_ref, xn_ref = refs[:3]
        refs = refs[3:]
    else:
        x_ref = refs[0]
        refs = refs[1:]
    dw_ref, dwb_ref, wa_ref, wb_ref, wo_ref, mod_ref, gpre_ref, gpost_ref, o_ref, gated_ref = refs
    t = pl.program_id(1)
    nt = pl.num_programs(1)
    tn = x_ref.shape[1]
    f = wa_ref.shape[1]
    nrows = tn // gw
    x = x_ref[0]
    gpre = gpre_ref[...]
    hb = _norm_mod(x, gpre, mod_ref, SH2, SC2).astype(BF16)
    if vertical:
        before = jnp.where(t > 0, _norm_mod(xp_ref[0], gpre, mod_ref, SH2, SC2), 0.0).astype(BF16)
        after = jnp.where(t < nt - 1, _norm_mod(xn_ref[0], gpre, mod_ref, SH2, SC2), 0.0).astype(BF16)
        hb_ext = jnp.concatenate([before, hb, after], axis=0)
    else:
        hb_ext = hb
    y = None
    strips = [slice(f0, min(f0 + FFN_LANES, f)) for f0 in range(0, f, FFN_LANES)]
    groups = [slice(g0, min(g0 + FFN_GROUP, f)) for g0 in range(0, f, FFN_GROUP)]

    a_next, gate_next = _dot(hb_ext, wa_ref[:, strips[0]]), _dot(hb, wb_ref[:, strips[0]])
    for si, fs in enumerate(strips):
        fl = fs.stop - fs.start
        a_ext, gate = a_next, gate_next
        sub = lax.broadcasted_iota(jnp.int32, (V7X_SUBLANES, fl), 0)
        w = [dw_ref[k:k + 1, fs] for k in range(dw_ref.shape[0])]
        bias = dwb_ref[:, fs]
        for i in range(nrows):
            if si + 1 < len(strips):
                if i == 0:
                    a_next = _dot(hb_ext, wa_ref[:, strips[si + 1]])
                if i == (nrows - 1) // 3:
                    gate_next = _dot(hb, wb_ref[:, strips[si + 1]])
            if i == 2 * (nrows - 1) // 3 and fs.start % FFN_GROUP == 0 and fs.start > 0:
                gs = groups[fs.start // FFN_GROUP - 1]
                part = _dot(gated_ref[:, gs], wo_ref[gs, :])
                y = part if y is None else y + part
            rs = slice(i * gw, (i + 1) * gw)
            if vertical:
                up, mid, dn = [a_ext[(i + j) * gw:(i + j + 1) * gw] for j in range(3)]
                left, centre, right = [w[j] * up + w[3 + j] * mid + w[6 + j] * dn for j in range(3)]
            else:
                mid = a_ext[rs]
                left, centre, right = [w[j] * mid for j in range(3)]
            from_left = pltpu.roll(left, 1, 0)
            from_left = jnp.concatenate([jnp.where(sub > 0, from_left[:V7X_SUBLANES], 0.0),
                                         from_left[V7X_SUBLANES:]], axis=0)
            from_right = pltpu.roll(right, gw - 1, 0)
            from_right = jnp.concatenate([from_right[:gw - V7X_SUBLANES],
                                          jnp.where(sub < V7X_SUBLANES - 1, from_right[gw - V7X_SUBLANES:], 0.0)],
                                         axis=0)
            conv = centre + bias + from_left + from_right
            gated_ref[rs, fs] = (_silu(conv) * gate[rs]).astype(BF16)
    part = _dot(gated_ref[:, groups[-1]], wo_ref[groups[-1], :])
    y = part if y is None else y + part
    o_ref[0] = _residual(x, y, gpost_ref[...], mod_ref, G2)


def _ffn_call(x, mods, dw, dwb, wa, wb, wo, gpre, gpost, tn, gw, vertical):
    bsz, n, d = x.shape
    f = wa.shape[-1]
    assert tn % gw == 0
    hb = tn // gw
    last = n // gw - 1
    tile_d = pl.BlockSpec((1, tn, d), lambda b, t: (b, t, 0))
    in_specs = [tile_d]
    args = [x]
    if vertical:
        in_specs += [pl.BlockSpec((1, gw, d), lambda b, t: (b, jnp.maximum(t * hb - 1, 0), 0)),
                     pl.BlockSpec((1, gw, d), lambda b, t: (b, jnp.minimum((t + 1) * hb, last), 0))]
        args += [x, x]
    in_specs += [_const_spec(dw.shape), _const_spec((1, f)), _const_spec((d, f)), _const_spec((d, f)),
                 _const_spec((f, d)), pl.BlockSpec((1, 6, d), lambda b, t: (b, 0, 0)), _const_spec((1, d)),
                 _const_spec((1, d))]
    args += [dw, dwb, wa, wb, wo, mods, gpre, gpost]
    return pl.pallas_call(
        functools.partial(_ffn_kernel, gw=gw, vertical=vertical),
        grid=(bsz, n // tn),
        in_specs=in_specs,
        out_specs=tile_d,
        out_shape=jax.ShapeDtypeStruct((bsz, n, d), F32),
        scratch_shapes=[pltpu.VMEM((tn, f), BF16)],
        compiler_params=_params("parallel", "parallel"),
        name="ffn_grid" if vertical else "ffn_seq",
    )(*args)


def _log_sigmoid(z):
    return jnp.minimum(z, 0.0) - jnp.log(1.0 + jnp.exp(-jnp.abs(z)))


def _gla_proj_kernel(x_ref, mod_ref, g_ref, wq_ref, wk_ref, wv_ref, wr_ref, wg1_ref, wg2_ref, bg_ref,
                     q_ref, k_ref, v_ref, r_ref, cf_ref, cb_ref, *, q_scale):
    tn = x_ref.shape[1]
    dk = q_ref.shape[-1]
    hb = _norm_mod(x_ref[0], g_ref[...], mod_ref, SH1, SC1).astype(BF16)
    low = _dot(hb, wg1_ref[...]).astype(BF16)
    z = _dot(low, wg2_ref[...]) + bg_ref[...]
    q_ref[0] = _dot(hb, wq_ref[...]) * q_scale
    k_ref[0] = _dot(hb, wk_ref[...])
    g = _log_sigmoid(z) * (LOG2_E / GATE_TAU)
    v_ref[0] = _dot(hb, wv_ref[...])
    r_ref[0] = _silu(_dot(hb, wr_ref[...])).astype(r_ref.dtype)
    g_hi = g.astype(BF16)
    rest = g - g_hi.astype(F32)
    g_mid = rest.astype(BF16)
    g_lo = (rest - g_mid.astype(F32)).astype(BF16)
    row = lax.broadcasted_iota(jnp.int32, (CHUNK, 3 * CHUNK), 0)
    colm = lax.broadcasted_iota(jnp.int32, (CHUNK, 3 * CHUNK), 1) & (CHUNK - 1)
    tri_f = jnp.where(colm <= row, 1.0, 0.0).astype(BF16)
    tri_b = jnp.where(colm >= row, 1.0, 0.0).astype(BF16)
    for c0 in range(0, tn, CHUNK):
        cs = slice(c0, c0 + CHUNK)
        pieces = jnp.concatenate([g_hi[cs], g_mid[cs], g_lo[cs]], axis=0)
        cf_ref[0, cs, :] = _dot(tri_f, pieces[:, :dk])
        cb_ref[0, cs, :] = _dot(tri_b, pieces[:, dk:])


def _gla_proj_call(x, mods, g, wq, wk, wv, wr, wg1, wg2, bg, tn):
    bsz, n, d = x.shape
    dk, dv = wq.shape[1], wv.shape[1]
    tile = lambda w: pl.BlockSpec((1, tn, w), lambda b, t: (b, t, 0))
    shp = lambda w: jax.ShapeDtypeStruct((bsz, n, w), F32)
    return pl.pallas_call(
        functools.partial(_gla_proj_kernel, q_scale=float((dk // GLA_HEADS) ** -0.5)),
        grid=(bsz, n // tn),
        in_specs=[tile(d), pl.BlockSpec((1, 6, d), lambda b, t: (b, 0, 0)), _const_spec((1, d)),
                  _const_spec(wq.shape), _const_spec(wk.shape), _const_spec(wv.shape), _const_spec(wr.shape),
                  _const_spec(wg1.shape), _const_spec(wg2.shape), _const_spec(bg.shape)],
        out_specs=[tile(dk), tile(dk), tile(dv), tile(dv), tile(dk), tile(dk)],
        out_shape=[shp(dk), shp(dk), shp(dv), jax.ShapeDtypeStruct((bsz, n, dv), BF16), shp(dk), shp(dk)],
        compiler_params=_params("parallel", "parallel"),
        name="gla_proj",
    )(x, mods, g, wq, wk, wv, wr, wg1, wg2, bg)


CHUNK_LEVELS = 6
assert 1 << CHUNK_LEVELS == CHUNK


def _level_reference(c_ref, base, level, reverse):
    half = 1 << level
    blk = 2 * half
    off = half if reverse else half - 1
    dk = c_ref.shape[-1]

    def row(r):
        return c_ref[0, pl.ds(base + (r + off), 1), :]

    if blk >= V7X_SUBLANES:
        return jnp.concatenate([jnp.broadcast_to(row(b0), (blk, dk)) for b0 in range(0, CHUNK, blk)], axis=0)
    sub = lax.broadcasted_iota(jnp.int32, (V7X_SUBLANES, dk), 0)
    groups = []
    for g0 in range(0, CHUNK, V7X_SUBLANES):
        r = jnp.broadcast_to(row(g0), (V7X_SUBLANES, dk))
        for b0 in range(blk, V7X_SUBLANES, blk):
            r = jnp.where(sub >= b0, jnp.broadcast_to(row(g0 + b0), (V7X_SUBLANES, dk)), r)
        groups.append(r)
    return jnp.concatenate(groups, axis=0)


def _gla_scan_kernel(q_ref, k_ref, v_ref, cf_ref, cb_ref, sf0_ref, sb0_ref, o_ref, sff_ref, sbf_ref,
                     stf_ref, stb_ref, incf_ref, incb_ref, *, nchunks):
    stf_ref[...] = sf0_ref[0, 0]
    stb_ref[...] = sb0_ref[0, 0]
    dk = q_ref.shape[-1]
    ti = lax.broadcasted_iota(jnp.int32, (CHUNK, CHUNK), 0)
    si = lax.broadcasted_iota(jnp.int32, (CHUNK, CHUNK), 1)
    x = ti ^ si
    lvl = sum([(x >= (1 << b)).astype(jnp.int32) for b in range(CHUNK_LEVELS)]) - 1
    row = lax.broadcasted_iota(jnp.int32, (CHUNK, dk), 0)
    below_split = [((row >> level) & 1) == 1 for level in range(CHUNK_LEVELS)]

    def chunk_scores(base):
        rows = pl.ds(base, CHUNK)
        q, k, cf, cb = q_ref[0, rows, :], k_ref[0, rows, :], cf_ref[0, rows, :], cb_ref[0, rows, :]
        kb = k.astype(BF16)
        scores = None
        for level in range(CHUNK_LEVELS):
            half = 1 << level
            if level == 0:
                xq = jnp.where(below_split[0], cf - pltpu.roll(cf, 1, 0), cb - pltpu.roll(cb, CHUNK - 1, 0))
                part = _dot_nt((q * jnp.exp2(xq)).astype(BF16), kb)
                scores = jnp.where(lvl == 0, part, 0.0)
                continue
            if half >= V7X_SUBLANES:
                xq, xk = [], []
                for b0 in range(0, CHUNK, 2 * half):
                    above, below = slice(b0, b0 + half), slice(b0 + half, b0 + 2 * half)
                    rf = cf_ref[0, pl.ds(base + (b0 + half - 1), 1), :]
                    rb = cb_ref[0, pl.ds(base + (b0 + half), 1), :]
                    xq += [cb[above] - rb, cf[below] - rf]
                    xk += [rf - cf[above], rb - cb[below]]
                xq, xk = jnp.concatenate(xq, axis=0), jnp.concatenate(xk, axis=0)
            else:
                df = cf - _level_reference(cf_ref, base, level, False)
                db = cb - _level_reference(cb_ref, base, level, True)
                xq = jnp.where(below_split[level], df, db)
                xk = -jnp.where(below_split[level], db, df)
            part = _dot_nt((q * jnp.exp2(xq)).astype(BF16), (k * jnp.exp2(xk)).astype(BF16))
            scores = jnp.where(lvl == level, part, scores)
        return scores.astype(BF16)

    def state_increment(base, c_ref, end_row):
        rows = pl.ds(base, CHUNK)
        c_end = c_ref[0, pl.ds(base + end_row, 1), :]
        kd = (k_ref[0, rows, :] * jnp.exp2(c_end - c_ref[0, rows, :])).astype(BF16)
        return _dot_tn(kd, v_ref[0, rows, :].astype(BF16))

    def carried(base, c_ref, end_row, st_ref, inc):
        rows = pl.ds(base, CHUNK)
        st = st_ref[...]
        o = _dot((q_ref[0, rows, :] * jnp.exp2(c_ref[0, rows, :])).astype(BF16), st.astype(BF16))
        c_end = c_ref[0, pl.ds(base + end_row, 1), :]
        decay = jnp.broadcast_to(jnp.exp2(c_end), (dk, dk)).T[:, 0:1]
        st_ref[...] = st * decay + inc
        return o

    def chunk_base(ci):
        return pl.multiple_of(ci * CHUNK, CHUNK)

    def produce(i, slot):
        fbase = chunk_base(jnp.minimum(i, nchunks - 1))
        bbase = chunk_base(jnp.maximum(nchunks - 1 - i, 0))
        incf_ref[slot] = state_increment(fbase, cf_ref, CHUNK - 1)
        incb_ref[slot] = state_increment(bbase, cb_ref, 0)
        return chunk_scores(fbase)

    def step(i, scores, first, slot):
        next_scores = produce(i + 1, 1 - slot)
        fbase, bbase = chunk_base(i), chunk_base(nchunks - 1 - i)
        frows, brows = pl.ds(fbase, CHUNK), pl.ds(bbase, CHUNK)
        q, k, v = q_ref[0, frows, :], k_ref[0, frows, :], v_ref[0, frows, :]
        diag = jnp.sum(q * k, axis=-1, keepdims=True)
        o_f = (_dot(scores, v.astype(BF16)) + (2.0 * diag) * v
               + carried(fbase, cf_ref, CHUNK - 1, stf_ref, incf_ref[slot]))
        o_b = carried(bbase, cb_ref, 0, stb_ref, incb_ref[slot])
        for rows, part in ((frows, o_f), (brows, o_b)):
            if not first:
                part = part + o_ref[0, rows, :].astype(F32)
            o_ref[0, rows, :] = part.astype(o_ref.dtype)
        return next_scores

    half = nchunks // 2
    per_trip = 4 if half % 4 == 0 else 2

    def trip(first):
        def body(j, s):
            for u in range(per_trip):
                s = step(per_trip * j + u, s, first, u % 2)
            return s
        return body

    scores = produce(0, 0)
    scores = lax.fori_loop(0, half // per_trip, trip(True), scores)
    lax.fori_loop(half // per_trip, nchunks // per_trip, trip(False), scores)
    sff_ref[0, 0] = stf_ref[...]
    sbf_ref[0, 0] = stb_ref[...]


def _gla_scan_call(q, k, v, cf, cb, s_f, s_b):
    bsz, n, dk = q.shape
    dv = v.shape[-1]
    hk, hv = dk // GLA_HEADS, dv // GLA_HEADS
    nchunks = n // CHUNK
    assert nchunks % 4 == 0
    tile = lambda w: pl.BlockSpec((1, n, w), lambda b, h: (b, 0, h))
    state = pl.BlockSpec((1, 1, hk, hv), lambda b, h: (b, h, 0, 0))
    state_shape = jax.ShapeDtypeStruct((bsz, GLA_HEADS, hk, hv), F32)
    return pl.pallas_call(
        functools.partial(_gla_scan_kernel, nchunks=nchunks),
        grid=(bsz, GLA_HEADS),
        in_specs=[tile(hk), tile(hk), tile(hv), tile(hk), tile(hk), state, state],
        out_specs=[tile(hv), state, state],
        out_shape=[jax.ShapeDtypeStruct((bsz, n, dv), BF16), state_shape, state_shape],
        scratch_shapes=[pltpu.VMEM((hk, hv), F32), pltpu.VMEM((hk, hv), F32),
                        pltpu.VMEM((2, hk, hv), F32), pltpu.VMEM((2, hk, hv), F32)],
        compiler_params=_params("parallel", "parallel"),
        name="gla_scan",
    )(q, k, v, cf, cb, s_f, s_b)


def _gla_out_kernel(o_in_ref, r_ref, ng_ref, wo_ref, x_ref, mod_ref, gpost_ref, o_ref, *, heads):
    o = o_in_ref[0].astype(F32)
    hv = o.shape[-1] // heads
    parts = []
    for h in range(heads):
        oh = o[:, h * hv:(h + 1) * hv]
        parts.append(oh * lax.rsqrt(jnp.mean(oh * oh, axis=-1, keepdims=True) + EPS))
    gated = jnp.concatenate(parts, axis=-1) * ng_ref[...] * r_ref[0].astype(F32)
    y = _dot(gated.astype(BF16), wo_ref[...])
    o_ref[0] = _residual(x_ref[0], y, gpost_ref[...], mod_ref, G1)


def _gla_out_call(o, r, ng, wo, x, mods, gpost, tn):
    bsz, n, d = x.shape
    dv = o.shape[-1]
    tile_v = pl.BlockSpec((1, tn, dv), lambda b, t: (b, t, 0))
    tile_d = pl.BlockSpec((1, tn, d), lambda b, t: (b, t, 0))
    return pl.pallas_call(
        functools.partial(_gla_out_kernel, heads=GLA_HEADS),
        grid=(bsz, n // tn),
        in_specs=[tile_v, tile_v, _const_spec((1, dv)), _const_spec((dv, d)), tile_d,
                  pl.BlockSpec((1, 6, d), lambda b, t: (b, 0, 0)), _const_spec((1, d))],
        out_specs=tile_d,
        out_shape=jax.ShapeDtypeStruct((bsz, n, d), F32),
        compiler_params=_params("parallel", "parallel"),
        name="gla_out",
    )(o, r, ng, wo, x, mods, gpost)


def _tile(n, want):
    return min(n, want)


def _conformer(x, mods, g_pre, g_post, w):
    return _cf_call(x, mods, g_pre, g_post, w["w1"], w["b1"], w["dw"], w["dwb"], w["ln_g"], w["ln_b"], w["w2"],
                    w["b2"], _tile(x.shape[1], 512))


def _gla_states(h, mods, g_pre, w, s_f, s_b):
    n = h.shape[1]
    q, k, v, r, cf, cb = _gla_proj_call(h, mods, g_pre, w["wq"], w["wk"], w["wv"], w["wr"], w["wg1"], w["wg2"],
                                        w["bg"], _tile(n, 512))
    o, fin_f, fin_b = _gla_scan_call(q, k, v, cf, cb, s_f, s_b)
    return o, r, fin_f, fin_b


def _gla_readout(x, mods, g_post, w, o, r):
    return _gla_out_call(o, r, w["norm_g"], w["wo"], x, mods, g_post, _tile(x.shape[1], 512))


def _conv_ffn(x, mods, g_pre, g_post, w, on_grid):
    n = x.shape[1]
    if on_grid:
        return _ffn_call(x, mods, w["dw"], w["dwb"], w["wa"], w["wb"], w["wo"], g_pre, g_post, _tile(n, 512),
                         GRID_W, True)
    return _ffn_call(x, mods, w["dw"][3:6], w["dwb"], w["wa"], w["wb"], w["wo"], g_pre, g_post, n, n, False)


def kernel(x, c, ctx, c_ctx, ada_w, ada_b, norm_pre_mix, norm_post_mix, norm_pre_ffn, norm_post_ffn, cf_w1, cf_b1, cf_dw, cf_dwb, cf_ln_g, cf_ln_b, cf_w2, cf_b2, gla_wq, gla_wk, gla_wv, gla_wr, gla_wg1, gla_wg2, gla_bg, gla_norm_g, gla_wo, ffn_wa, ffn_wb, ffn_dw, ffn_dwb, ffn_wo):
    bsz, n, d = x.shape
    depth = ada_w.shape[0]
    dk = gla_wq.shape[-1]
    dv = gla_wv.shape[-1]
    hk, hv = dk // GLA_HEADS, dv // GLA_HEADS
    row = lambda v: v.reshape(1, -1)

    cond_rows = -(-(bsz + 1) // V7X_SUBLANES) * V7X_SUBLANES
    cond = jnp.zeros((cond_rows, d), F32).at[:bsz].set(c).at[bsz].set(c_ctx)
    mods = _ada_call(cond, ada_w, ada_b)

    for i in range(depth):
        last = i == depth - 1
        j = i // N_MIXERS
        mx = mods[i, :bsz].reshape(bsz, 6, d)
        mc = jnp.broadcast_to(mods[i, bsz].reshape(1, 6, d), (bsz, 6, d))
        g_pre, g_post = row(norm_pre_mix[i]), row(norm_post_mix[i])
        if i % N_MIXERS == 0:
            w = dict(w1=cf_w1[j].astype(BF16), b1=row(cf_b1[j]), dw=cf_dw[j], dwb=row(cf_dwb[j]),
                     ln_g=row(cf_ln_g[j]), ln_b=row(cf_ln_b[j]), w2=cf_w2[j].astype(BF16), b2=row(cf_b2[j]))
            x = _conformer(x, mx, g_pre, g_post, w)
            if not last:
                ctx_mixed = _conformer(ctx, mc, g_pre, g_post, w)
        else:
            rank = gla_wg1.shape[-1]
            wg2 = jnp.zeros((2 * rank, 2 * dk), F32)
            wg2 = wg2.at[:rank, :dk].set(gla_wg2[j, 0]).at[rank:, dk:].set(gla_wg2[j, 1])
            w = dict(wq=gla_wq[j].astype(BF16), wk=gla_wk[j].astype(BF16), wv=gla_wv[j].astype(BF16),
                     wr=gla_wr[j].astype(BF16),
                     wg1=jnp.concatenate([gla_wg1[j, 0], gla_wg1[j, 1]], axis=-1).astype(BF16),
                     wg2=wg2.astype(BF16), bg=gla_bg[j].reshape(1, 2 * dk),
                     norm_g=jnp.tile(gla_norm_g[j], GLA_HEADS).reshape(1, dv), wo=gla_wo[j].astype(BF16))
            zeros = jnp.zeros((bsz, GLA_HEADS, hk, hv), F32)
            oc, rc, s_f, s_b = _gla_states(ctx, mc, g_pre, w, zeros, zeros)
            if not last:
                ctx_mixed = _gla_readout(ctx, mc, g_post, w, oc, rc)
            ox, rx, _, _ = _gla_states(x, mx, g_pre, w, s_f, s_b)
            x = _gla_readout(x, mx, g_post, w, ox, rx)
        fw = dict(wa=ffn_wa[i].astype(BF16), wb=ffn_wb[i].astype(BF16), dw=ffn_dw[i].reshape(-1, ffn_dw.shape[-1]),
                  dwb=row(ffn_dwb[i]), wo=ffn_wo[i].astype(BF16))
        g_pre_f, g_post_f = row(norm_pre_ffn[i]), row(norm_post_ffn[i])
        x = _conv_ffn(x, mx, g_pre_f, g_post_f, fw, True)
        if not last:
            ctx = _conv_ffn(ctx_mixed, mc, g_pre_f, g_post_f, fw, False)
    return x
```

```python
import functools

import jax
import jax.numpy as jnp
from jax import lax
from jax.experimental import pallas as pl
from jax.experimental.pallas import tpu as pltpu

EPS = 1e-6
N_MIXERS = 2
GRID_W = 64
GLA_HEADS = 4
GATE_TAU = 16.0
CHUNK = 64
LOG2_E = 1.4426950408889634

V7X_LANES = 128
V7X_SUBLANES = 8
V7X_VMEM_BYTES = 64 * 1024 * 1024
VMEM_LIMIT = V7X_VMEM_BYTES * 7 // 8

F32 = jnp.float32
BF16 = jnp.bfloat16

SH1, SC1, G1, SH2, SC2, G2 = range(6)


def _params(*sem):
    return pltpu.CompilerParams(dimension_semantics=sem, vmem_limit_bytes=VMEM_LIMIT)


def _const_spec(shape):
    nd = len(shape)
    return pl.BlockSpec(shape, lambda *_: (0,) * nd, pipeline_mode=pl.Buffered(1))


def _rms(x, g):
    return x * lax.rsqrt(jnp.mean(x * x, axis=-1, keepdims=True) + EPS) * g


def _silu(x):
    return x * jax.nn.sigmoid(x)


def _norm_mod(x, g, mod_ref, shift_row, scale_row):
    h = _rms(x, g)
    return h * (1.0 + mod_ref[0, scale_row:scale_row + 1, :]) + mod_ref[0, shift_row:shift_row + 1, :]


def _residual(x, y, gpost, mod_ref, gate_row):
    return x + mod_ref[0, gate_row:gate_row + 1, :] * _rms(y, gpost)


def _dot(a, b):
    return jnp.dot(a, b, preferred_element_type=F32)


def _dot_nt(a, b):
    return lax.dot_general(a, b, (((1,), (1,)), ((), ())), preferred_element_type=F32)


def _dot_tn(a, b):
    return lax.dot_general(a, b, (((0,), (0,)), ((), ())), preferred_element_type=F32)


def _ada_kernel(cond_ref, w_ref, b_ref, o_ref):
    s = _silu(cond_ref[...])
    o_ref[0] = jnp.dot(s, w_ref[0], preferred_element_type=F32, precision=lax.Precision.HIGHEST) + b_ref[0]


def _ada_call(cond, ada_w, ada_b):
    depth, d, d6 = ada_w.shape
    rows = cond.shape[0]
    tn = d6 // 8
    return pl.pallas_call(
        _ada_kernel,
        grid=(depth, d6 // tn),
        in_specs=[
            pl.BlockSpec((rows, d), lambda i, j: (0, 0)),
            pl.BlockSpec((1, d, tn), lambda i, j: (i, 0, j)),
            pl.BlockSpec((1, 1, tn), lambda i, j: (i, 0, j)),
        ],
        out_specs=pl.BlockSpec((1, rows, tn), lambda i, j: (i, 0, j)),
        out_shape=jax.ShapeDtypeStruct((depth, rows, d6), F32),
        compiler_params=_params("parallel", "parallel"),
        name="ada",
    )(cond, ada_w, ada_b.reshape(depth, 1, d6))


CF_HALO = 16
CF_ROWS = 128
CF_LANES = 2 * V7X_LANES


def _cf_kernel(x_ref, xp_ref, xn_ref, mod_ref, gpre_ref, gpost_ref, w1_ref, b1_ref, dw_ref, dwb_ref, lng_ref,
               lnb_ref, w2_ref, b2_ref, o_ref, buf_ref, u_ref, *, width):
    t = pl.program_id(1)
    nt = pl.num_programs(1)
    tn, d = x_ref.shape[1:]
    pad = width // 2
    x = x_ref[0]
    gpre = gpre_ref[...]
    hb_ext = jnp.concatenate([_norm_mod(v, gpre, mod_ref, SH1, SC1).astype(BF16)
                              for v in (xp_ref[0], x, xn_ref[0])], axis=0)

    def stage_glu(cs):
        gate_cs = slice(d + cs.start, d + cs.stop)
        a = _dot(hb_ext, w1_ref[:, cs]) + b1_ref[:, cs]
        gt = _dot(hb_ext, w1_ref[:, gate_cs]) + b1_ref[:, gate_cs]
        glu = a * jax.nn.sigmoid(gt)
        buf_ref[0:CF_HALO, cs] = jnp.where(t > 0, glu[0:CF_HALO], 0.0)
        buf_ref[CF_HALO:CF_HALO + tn, cs] = glu[CF_HALO:CF_HALO + tn]
        buf_ref[CF_HALO + tn:, cs] = jnp.where(t < nt - 1, glu[CF_HALO + tn:], 0.0)

    lead = CF_HALO - pad
    groups = -(-(lead + width) // V7X_SUBLANES)
    strips = [slice(c0, c0 + CF_LANES) for c0 in range(0, d, CF_LANES)]
    stage_glu(strips[0])
    for si, cs in enumerate(strips):
        if si + 1 < len(strips):
            stage_glu(strips[si + 1])
        for l0 in range(cs.start, cs.stop, V7X_LANES):
            ls = slice(l0, l0 + V7X_LANES)
            for r0 in range(0, tn, CF_ROWS):
                acc = None
                for r in range(V7X_SUBLANES):
                    aligned = None
                    for a in range(groups):
                        k = V7X_SUBLANES * a + r - lead
                        if 0 <= k < width:
                            lo = r0 + V7X_SUBLANES * a
                            term = buf_ref[lo:lo + CF_ROWS + V7X_SUBLANES, ls] * dw_ref[k:k + 1, ls]
                            aligned = term if aligned is None else aligned + term
                    piece = aligned[r:r + CF_ROWS]
                    acc = piece if acc is None else acc + piece
                u_ref[r0:r0 + CF_ROWS, ls] = acc
    u = u_ref[...] + dwb_ref[...]
    mu = jnp.mean(u, axis=-1, keepdims=True)
    uc = u - mu
    var = jnp.mean(uc * uc, axis=-1, keepdims=True)
    v = _silu(uc * lax.rsqrt(var + EPS) * lng_ref[...] + lnb_ref[...])
    y = _dot(v.astype(BF16), w2_ref[...]) + b2_ref[...]
    o_ref[0] = _residual(x, y, gpost_ref[...], mod_ref, G1)


def _cf_call(x, mods, gpre, gpost, w1, b1, dw, dwb, lng, lnb, w2, b2, tn):
    bsz, n, d = x.shape
    width = dw.shape[0]
    assert width // 2 < CF_HALO and tn % CF_ROWS == 0 and d % CF_LANES == 0
    hb = tn // CF_HALO
    last = n // CF_HALO - 1
    tile = pl.BlockSpec((1, tn, d), lambda b, t: (b, t, 0))
    row = _const_spec((1, d))
    return pl.pallas_call(
        functools.partial(_cf_kernel, width=width),
        grid=(bsz, n // tn),
        in_specs=[
            tile,
            pl.BlockSpec((1, CF_HALO, d), lambda b, t: (b, jnp.maximum(t * hb - 1, 0), 0)),
            pl.BlockSpec((1, CF_HALO, d), lambda b, t: (b, jnp.minimum((t + 1) * hb, last), 0)),
            pl.BlockSpec((1, 6, d), lambda b, t: (b, 0, 0)),
            row, row, _const_spec((d, 2 * d)), _const_spec((1, 2 * d)), _const_spec((width, d)), row, row, row,
            _const_spec((d, d)), row,
        ],
        out_specs=tile,
        out_shape=jax.ShapeDtypeStruct((bsz, n, d), F32),
        scratch_shapes=[pltpu.VMEM((tn + 2 * CF_HALO, d), F32), pltpu.VMEM((tn, d), F32)],
        compiler_params=_params("parallel", "parallel"),
        name="cf",
    )(x, x, x, mods, gpre, gpost, w1, b1, dw, dwb, lng, lnb, w2, b2)


FFN_LANES = 8 * V7X_LANES


def _ffn_strips(f):
    return [slice(f0, min(f0 + FFN_LANES, f)) for f0 in range(0, f, FFN_LANES)]


def _ffn_kernel(*refs, gw, vertical):
    if vertical:
        x_ref, xp_ref, xn_ref = refs[:3]
        refs = refs[3:]
    else:
        x_ref = refs[0]
        refs = refs[1:]
    dw_ref, dwb_ref, wa_ref, wb_ref, wo_ref, mod_ref, gpre_ref, gpost_ref, o_ref, gated_ref = refs
    t = pl.program_id(1)
    nt = pl.num_programs(1)
    tn = x_ref.shape[1]
    f = wa_ref.shape[1]
    nrows = tn // gw
    x = x_ref[0]
    gpre = gpre_ref[...]
    hb = _norm_mod(x, gpre, mod_ref, SH2, SC2).astype(BF16)
    if vertical:
        before = jnp.where(t > 0, _norm_mod(xp_ref[0], gpre, mod_ref, SH2, SC2), 0.0).astype(BF16)
        after = jnp.where(t < nt - 1, _norm_mod(xn_ref[0], gpre, mod_ref, SH2, SC2), 0.0).astype(BF16)
        hb_ext = jnp.concatenate([before, hb, after], axis=0)
    else:
        hb_ext = hb
    y = None
    strips = _ffn_strips(f)

    a_next, gate_next = _dot(hb_ext, wa_ref[:, strips[0]]), _dot(hb, wb_ref[:, strips[0]])
    for si, fs in enumerate(strips):
        fl = fs.stop - fs.start
        a_ext, gate = a_next, gate_next
        sub = lax.broadcasted_iota(jnp.int32, (V7X_SUBLANES, fl), 0)
        w = [dw_ref[k:k + 1, fs] for k in range(dw_ref.shape[0])]
        bias = dwb_ref[:, fs]
        if si + 1 < len(strips):
            a_next = _dot(hb_ext, wa_ref[:, strips[si + 1]])
            gate_next = _dot(hb, wb_ref[:, strips[si + 1]])
        if si > 0:
            part = _dot(gated_ref[:, strips[si - 1]], wo_ref[strips[si - 1], :])
            y = part if y is None else y + part
        for i in range(nrows):
            rs = slice(i * gw, (i + 1) * gw)
            if vertical:
                up, mid, dn = [a_ext[(i + j) * gw:(i + j + 1) * gw] for j in range(3)]
                left, centre, right = [w[j] * up + w[3 + j] * mid + w[6 + j] * dn for j in range(3)]
            else:
                mid = a_ext[rs]
                left, centre, right = [w[j] * mid for j in range(3)]
            from_left = pltpu.roll(left, 1, 0)
            from_left = jnp.concatenate([jnp.where(sub > 0, from_left[:V7X_SUBLANES], 0.0),
                                         from_left[V7X_SUBLANES:]], axis=0)
            from_right = pltpu.roll(right, gw - 1, 0)
            from_right = jnp.concatenate([from_right[:gw - V7X_SUBLANES],
                                          jnp.where(sub < V7X_SUBLANES - 1, from_right[gw - V7X_SUBLANES:], 0.0)],
                                         axis=0)
            conv = centre + bias + from_left + from_right
            gated_ref[rs, fs] = (_silu(conv) * gate[rs]).astype(BF16)
    part = _dot(gated_ref[:, strips[-1]], wo_ref[strips[-1], :])
    y = part if y is None else y + part
    o_ref[0] = _residual(x, y, gpost_ref[...], mod_ref, G2)


def _ffn_call(x, mods, dw, dwb, wa, wb, wo, gpre, gpost, tn, gw, vertical):
    bsz, n, d = x.shape
    f = wa.shape[-1]
    assert tn % gw == 0
    hb = tn // gw
    last = n // gw - 1
    tile_d = pl.BlockSpec((1, tn, d), lambda b, t: (b, t, 0))
    in_specs = [tile_d]
    args = [x]
    if vertical:
        in_specs += [pl.BlockSpec((1, gw, d), lambda b, t: (b, jnp.maximum(t * hb - 1, 0), 0)),
                     pl.BlockSpec((1, gw, d), lambda b, t: (b, jnp.minimum((t + 1) * hb, last), 0))]
        args += [x, x]
    in_specs += [_const_spec(dw.shape), _const_spec((1, f)), _const_spec((d, f)), _const_spec((d, f)),
                 _const_spec((f, d)), pl.BlockSpec((1, 6, d), lambda b, t: (b, 0, 0)), _const_spec((1, d)),
                 _const_spec((1, d))]
    args += [dw, dwb, wa, wb, wo, mods, gpre, gpost]
    return pl.pallas_call(
        functools.partial(_ffn_kernel, gw=gw, vertical=vertical),
        grid=(bsz, n // tn),
        in_specs=in_specs,
        out_specs=tile_d,
        out_shape=jax.ShapeDtypeStruct((bsz, n, d), F32),
        scratch_shapes=[pltpu.VMEM((tn, f), BF16)],
        compiler_params=_params("parallel", "parallel"),
        name="ffn_grid" if vertical else "ffn_seq",
    )(*args)


def _log_sigmoid(z):
    return jnp.minimum(z, 0.0) - jnp.log(1.0 + jnp.exp(-jnp.abs(z)))


def _gla_proj_kernel(x_ref, mod_ref, g_ref, wq_ref, wk_ref, wv_ref, wr_ref, wg1_ref, wg2_ref, bg_ref,
                     q_ref, k_ref, v_ref, r_ref, cf_ref, cb_ref, *, q_scale):
    tn = x_ref.shape[1]
    dk = q_ref.shape[-1]
    hb = _norm_mod(x_ref[0], g_ref[...], mod_ref, SH1, SC1).astype(BF16)
    low = _dot(hb, wg1_ref[...]).astype(BF16)
    z = _dot(low, wg2_ref[...]) + bg_ref[...]
    q_ref[0] = _dot(hb, wq_ref[...]) * q_scale
    k_ref[0] = _dot(hb, wk_ref[...])
    g = _log_sigmoid(z) * (LOG2_E / GATE_TAU)
    v_ref[0] = _dot(hb, wv_ref[...])
    r_ref[0] = _silu(_dot(hb, wr_ref[...])).astype(r_ref.dtype)
    g_hi = g.astype(BF16)
    rest = g - g_hi.astype(F32)
    g_mid = rest.astype(BF16)
    g_lo = (rest - g_mid.astype(F32)).astype(BF16)
    row = lax.broadcasted_iota(jnp.int32, (CHUNK, 3 * CHUNK), 0)
    colm = lax.broadcasted_iota(jnp.int32, (CHUNK, 3 * CHUNK), 1) & (CHUNK - 1)
    tri_f = jnp.where(colm <= row, 1.0, 0.0).astype(BF16)
    tri_b = jnp.where(colm >= row, 1.0, 0.0).astype(BF16)
    for c0 in range(0, tn, CHUNK):
        cs = slice(c0, c0 + CHUNK)
        pieces = jnp.concatenate([g_hi[cs], g_mid[cs], g_lo[cs]], axis=0)
        cf_ref[0, cs, :] = _dot(tri_f, pieces[:, :dk])
        cb_ref[0, cs, :] = _dot(tri_b, pieces[:, dk:])


def _gla_proj_call(x, mods, g, wq, wk, wv, wr, wg1, wg2, bg, tn):
    bsz, n, d = x.shape
    dk, dv = wq.shape[1], wv.shape[1]
    tile = lambda w: pl.BlockSpec((1, tn, w), lambda b, t: (b, t, 0))
    shp = lambda w: jax.ShapeDtypeStruct((bsz, n, w), F32)
    return pl.pallas_call(
        functools.partial(_gla_proj_kernel, q_scale=float((dk // GLA_HEADS) ** -0.5)),
        grid=(bsz, n // tn),
        in_specs=[tile(d), pl.BlockSpec((1, 6, d), lambda b, t: (b, 0, 0)), _const_spec((1, d)),
                  _const_spec(wq.shape), _const_spec(wk.shape), _const_spec(wv.shape), _const_spec(wr.shape),
                  _const_spec(wg1.shape), _const_spec(wg2.shape), _const_spec(bg.shape)],
        out_specs=[tile(dk), tile(dk), tile(dv), tile(dv), tile(dk), tile(dk)],
        out_shape=[shp(dk), shp(dk), shp(dv), jax.ShapeDtypeStruct((bsz, n, dv), BF16), shp(dk), shp(dk)],
        compiler_params=_params("parallel", "parallel"),
        name="gla_proj",
    )(x, mods, g, wq, wk, wv, wr, wg1, wg2, bg)


CHUNK_LEVELS = 6
assert 1 << CHUNK_LEVELS == CHUNK


def _level_reference(c_ref, base, level, reverse):
    half = 1 << level
    blk = 2 * half
    off = half if reverse else half - 1
    dk = c_ref.shape[-1]

    def row(r):
        return c_ref[0, pl.ds(base + (r + off), 1), :]

    if blk >= V7X_SUBLANES:
        return jnp.concatenate([jnp.broadcast_to(row(b0), (blk, dk)) for b0 in range(0, CHUNK, blk)], axis=0)
    sub = lax.broadcasted_iota(jnp.int32, (V7X_SUBLANES, dk), 0)
    groups = []
    for g0 in range(0, CHUNK, V7X_SUBLANES):
        r = jnp.broadcast_to(row(g0), (V7X_SUBLANES, dk))
        for b0 in range(blk, V7X_SUBLANES, blk):
            r = jnp.where(sub >= b0, jnp.broadcast_to(row(g0 + b0), (V7X_SUBLANES, dk)), r)
        groups.append(r)
    return jnp.concatenate(groups, axis=0)


def _gla_scan_kernel(q_ref, k_ref, v_ref, cf_ref, cb_ref, sf0_ref, sb0_ref, o_ref, sff_ref, sbf_ref,
                     stf_ref, stb_ref, incf_ref, incb_ref, *, nchunks):
    stf_ref[...] = sf0_ref[0, 0]
    stb_ref[...] = sb0_ref[0, 0]
    dk = q_ref.shape[-1]
    ti = lax.broadcasted_iota(jnp.int32, (CHUNK, CHUNK), 0)
    si = lax.broadcasted_iota(jnp.int32, (CHUNK, CHUNK), 1)
    x = ti ^ si
    lvl = sum([(x >= (1 << b)).astype(jnp.int32) for b in range(CHUNK_LEVELS)]) - 1
    row = lax.broadcasted_iota(jnp.int32, (CHUNK, dk), 0)
    below_split = [((row >> level) & 1) == 1 for level in range(CHUNK_LEVELS)]

    def chunk_scores(base):
        rows = pl.ds(base, CHUNK)
        q, k, cf, cb = q_ref[0, rows, :], k_ref[0, rows, :], cf_ref[0, rows, :], cb_ref[0, rows, :]
        kb = k.astype(BF16)
        scores = None
        for level in range(CHUNK_LEVELS):
            half = 1 << level
            if level == 0:
                xq = jnp.where(below_split[0], cf - pltpu.roll(cf, 1, 0), cb - pltpu.roll(cb, CHUNK - 1, 0))
                part = _dot_nt((q * jnp.exp2(xq)).astype(BF16), kb)
                scores = jnp.where(lvl == 0, part, 0.0)
                continue
            if half >= V7X_SUBLANES:
                xq, xk = [], []
                for b0 in range(0, CHUNK, 2 * half):
                    above, below = slice(b0, b0 + half), slice(b0 + half, b0 + 2 * half)
                    rf = cf_ref[0, pl.ds(base + (b0 + half - 1), 1), :]
                    rb = cb_ref[0, pl.ds(base + (b0 + half), 1), :]
                    xq += [cb[above] - rb, cf[below] - rf]
                    xk += [rf - cf[above], rb - cb[below]]
                xq, xk = jnp.concatenate(xq, axis=0), jnp.concatenate(xk, axis=0)
            else:
                df = cf - _level_reference(cf_ref, base, level, False)
                db = cb - _level_reference(cb_ref, base, level, True)
                xq = jnp.where(below_split[level], df, db)
                xk = -jnp.where(below_split[level], db, df)
            part = _dot_nt((q * jnp.exp2(xq)).astype(BF16), (k * jnp.exp2(xk)).astype(BF16))
            scores = jnp.where(lvl == level, part, scores)
        return scores.astype(BF16)

    def state_increment(base, c_ref, end_row):
        rows = pl.ds(base, CHUNK)
        c_end = c_ref[0, pl.ds(base + end_row, 1), :]
        kd = (k_ref[0, rows, :] * jnp.exp2(c_end - c_ref[0, rows, :])).astype(BF16)
        return _dot_tn(kd, v_ref[0, rows, :].astype(BF16))

    def carried(base, c_ref, end_row, st_ref, inc):
        rows = pl.ds(base, CHUNK)
        st = st_ref[...]
        o = _dot((q_ref[0, rows, :] * jnp.exp2(c_ref[0, rows, :])).astype(BF16), st.astype(BF16))
        c_end = c_ref[0, pl.ds(base + end_row, 1), :]
        decay = jnp.broadcast_to(jnp.exp2(c_end), (dk, dk)).T[:, 0:1]
        st_ref[...] = st * decay + inc
        return o

    def chunk_base(ci):
        return pl.multiple_of(ci * CHUNK, CHUNK)

    def produce(i, slot):
        fbase = chunk_base(jnp.minimum(i, nchunks - 1))
        bbase = chunk_base(jnp.maximum(nchunks - 1 - i, 0))
        incf_ref[slot] = state_increment(fbase, cf_ref, CHUNK - 1)
        incb_ref[slot] = state_increment(bbase, cb_ref, 0)
        return chunk_scores(fbase)

    def step(i, scores, first, slot):
        next_scores = produce(i + 1, 1 - slot)
        fbase, bbase = chunk_base(i), chunk_base(nchunks - 1 - i)
        frows, brows = pl.ds(fbase, CHUNK), pl.ds(bbase, CHUNK)
        q, k, v = q_ref[0, frows, :], k_ref[0, frows, :], v_ref[0, frows, :]
        diag = jnp.sum(q * k, axis=-1, keepdims=True)
        o_f = (_dot(scores, v.astype(BF16)) + (2.0 * diag) * v
               + carried(fbase, cf_ref, CHUNK - 1, stf_ref, incf_ref[slot]))
        o_b = carried(bbase, cb_ref, 0, stb_ref, incb_ref[slot])
        for rows, part in ((frows, o_f), (brows, o_b)):
            if not first:
                part = part + o_ref[0, rows, :].astype(F32)
            o_ref[0, rows, :] = part.astype(o_ref.dtype)
        return next_scores

    half = nchunks // 2
    per_trip = 4 if half % 4 == 0 else 2

    def trip(first):
        def body(j, s):
            for u in range(per_trip):
                s = step(per_trip * j + u, s, first, u % 2)
            return s
        return body

    scores = produce(0, 0)
    scores = lax.fori_loop(0, half // per_trip, trip(True), scores)
    lax.fori_loop(half // per_trip, nchunks // per_trip, trip(False), scores)
    sff_ref[0, 0] = stf_ref[...]
    sbf_ref[0, 0] = stb_ref[...]


def _gla_scan_call(q, k, v, cf, cb, s_f, s_b):
    bsz, n, dk = q.shape
    dv = v.shape[-1]
    hk, hv = dk // GLA_HEADS, dv // GLA_HEADS
    nchunks = n // CHUNK
    assert nchunks % 4 == 0
    tile = lambda w: pl.BlockSpec((1, n, w), lambda b, h: (b, 0, h))
    state = pl.BlockSpec((1, 1, hk, hv), lambda b, h: (b, h, 0, 0))
    state_shape = jax.ShapeDtypeStruct((bsz, GLA_HEADS, hk, hv), F32)
    return pl.pallas_call(
        functools.partial(_gla_scan_kernel, nchunks=nchunks),
        grid=(bsz, GLA_HEADS),
        in_specs=[tile(hk), tile(hk), tile(hv), tile(hk), tile(hk), state, state],
        out_specs=[tile(hv), state, state],
        out_shape=[jax.ShapeDtypeStruct((bsz, n, dv), BF16), state_shape, state_shape],
        scratch_shapes=[pltpu.VMEM((hk, hv), F32), pltpu.VMEM((hk, hv), F32),
                        pltpu.VMEM((2, hk, hv), F32), pltpu.VMEM((2, hk, hv), F32)],
        compiler_params=_params("parallel", "parallel"),
        name="gla_scan",
    )(q, k, v, cf, cb, s_f, s_b)


def _gla_out_kernel(o_in_ref, r_ref, ng_ref, wo_ref, x_ref, mod_ref, gpost_ref, o_ref, *, heads):
    o = o_in_ref[0].astype(F32)
    hv = o.shape[-1] // heads
    parts = []
    for h in range(heads):
        oh = o[:, h * hv:(h + 1) * hv]
        parts.append(oh * lax.rsqrt(jnp.mean(oh * oh, axis=-1, keepdims=True) + EPS))
    gated = jnp.concatenate(parts, axis=-1) * ng_ref[...] * r_ref[0].astype(F32)
    y = _dot(gated.astype(BF16), wo_ref[...])
    o_ref[0] = _residual(x_ref[0], y, gpost_ref[...], mod_ref, G1)


def _gla_out_call(o, r, ng, wo, x, mods, gpost, tn):
    bsz, n, d = x.shape
    dv = o.shape[-1]
    tile_v = pl.BlockSpec((1, tn, dv), lambda b, t: (b, t, 0))
    tile_d = pl.BlockSpec((1, tn, d), lambda b, t: (b, t, 0))
    return pl.pallas_call(
        functools.partial(_gla_out_kernel, heads=GLA_HEADS),
        grid=(bsz, n // tn),
        in_specs=[tile_v, tile_v, _const_spec((1, dv)), _const_spec((dv, d)), tile_d,
                  pl.BlockSpec((1, 6, d), lambda b, t: (b, 0, 0)), _const_spec((1, d))],
        out_specs=tile_d,
        out_shape=jax.ShapeDtypeStruct((bsz, n, d), F32),
        compiler_params=_params("parallel", "parallel"),
        name="gla_out",
    )(o, r, ng, wo, x, mods, gpost)


def _tile(n, want):
    return min(n, want)


def _conformer(x, mods, g_pre, g_post, w):
    return _cf_call(x, mods, g_pre, g_post, w["w1"], w["b1"], w["dw"], w["dwb"], w["ln_g"], w["ln_b"], w["w2"],
                    w["b2"], _tile(x.shape[1], 512))


def _gla_states(h, mods, g_pre, w, s_f, s_b):
    n = h.shape[1]
    q, k, v, r, cf, cb = _gla_proj_call(h, mods, g_pre, w["wq"], w["wk"], w["wv"], w["wr"], w["wg1"], w["wg2"],
                                        w["bg"], _tile(n, 512))
    o, fin_f, fin_b = _gla_scan_call(q, k, v, cf, cb, s_f, s_b)
    return o, r, fin_f, fin_b


def _gla_readout(x, mods, g_post, w, o, r):
    return _gla_out_call(o, r, w["norm_g"], w["wo"], x, mods, g_post, _tile(x.shape[1], 512))


def _conv_ffn(x, mods, g_pre, g_post, w, on_grid):
    n = x.shape[1]
    if on_grid:
        return _ffn_call(x, mods, w["dw"], w["dwb"], w["wa"], w["wb"], w["wo"], g_pre, g_post, _tile(n, 512),
                         GRID_W, True)
    return _ffn_call(x, mods, w["dw"][3:6], w["dwb"], w["wa"], w["wb"], w["wo"], g_pre, g_post, n, n, False)


def kernel(x, c, ctx, c_ctx, ada_w, ada_b, norm_pre_mix, norm_post_mix, norm_pre_ffn, norm_post_ffn, cf_w1, cf_b1, cf_dw, cf_dwb, cf_ln_g, cf_ln_b, cf_w2, cf_b2, gla_wq, gla_wk, gla_wv, gla_wr, gla_wg1, gla_wg2, gla_bg, gla_norm_g, gla_wo, ffn_wa, ffn_wb, ffn_dw, ffn_dwb, ffn_wo):
    bsz, n, d = x.shape
    depth = ada_w.shape[0]
    dk = gla_wq.shape[-1]
    dv = gla_wv.shape[-1]
    hk, hv = dk // GLA_HEADS, dv // GLA_HEADS
    row = lambda v: v.reshape(1, -1)

    cond_rows = -(-(bsz + 1) // V7X_SUBLANES) * V7X_SUBLANES
    cond = jnp.zeros((cond_rows, d), F32).at[:bsz].set(c).at[bsz].set(c_ctx)
    mods = _ada_call(cond, ada_w, ada_b)

    for i in range(depth):
        last = i == depth - 1
        j = i // N_MIXERS
        mx = mods[i, :bsz].reshape(bsz, 6, d)
        mc = jnp.broadcast_to(mods[i, bsz].reshape(1, 6, d), (bsz, 6, d))
        g_pre, g_post = row(norm_pre_mix[i]), row(norm_post_mix[i])
        if i % N_MIXERS == 0:
            w = dict(w1=cf_w1[j].astype(BF16), b1=row(cf_b1[j]), dw=cf_dw[j], dwb=row(cf_dwb[j]),
                     ln_g=row(cf_ln_g[j]), ln_b=row(cf_ln_b[j]), w2=cf_w2[j].astype(BF16), b2=row(cf_b2[j]))
            x = _conformer(x, mx, g_pre, g_post, w)
            if not last:
                ctx_mixed = _conformer(ctx, mc, g_pre, g_post, w)
        else:
            rank = gla_wg1.shape[-1]
            wg2 = jnp.zeros((2 * rank, 2 * dk), F32)
            wg2 = wg2.at[:rank, :dk].set(gla_wg2[j, 0]).at[rank:, dk:].set(gla_wg2[j, 1])
            w = dict(wq=gla_wq[j].astype(BF16), wk=gla_wk[j].astype(BF16), wv=gla_wv[j].astype(BF16),
                     wr=gla_wr[j].astype(BF16),
                     wg1=jnp.concatenate([gla_wg1[j, 0], gla_wg1[j, 1]], axis=-1).astype(BF16),
                     wg2=wg2.astype(BF16), bg=gla_bg[j].reshape(1, 2 * dk),
                     norm_g=jnp.tile(gla_norm_g[j], GLA_HEADS).reshape(1, dv), wo=gla_wo[j].astype(BF16))
            zeros = jnp.zeros((bsz, GLA_HEADS, hk, hv), F32)
            oc, rc, s_f, s_b = _gla_states(ctx, mc, g_pre, w, zeros, zeros)
            if not last:
                ctx_mixed = _gla_readout(ctx, mc, g_post, w, oc, rc)
            ox, rx, _, _ = _gla_states(x, mx, g_pre, w, s_f, s_b)
            x = _gla_readout(x, mx, g_post, w, ox, rx)
        fw = dict(wa=ffn_wa[i].astype(BF16), wb=ffn_wb[i].astype(BF16), dw=ffn_dw[i].reshape(-1, ffn_dw.shape[-1]),
                  dwb=row(ffn_dwb[i]), wo=ffn_wo[i].astype(BF16))
        g_pre_f, g_post_f = row(norm_pre_ffn[i]), row(norm_post_ffn[i])
        x = _conv_ffn(x, mx, g_pre_f, g_post_f, fw, True)
        if not last:
            ctx = _conv_ffn(ctx_mixed, mc, g_pre_f, g_post_f, fw, False)
    return x
```

```python
import functools

import jax
import jax.numpy as jnp
from jax import lax
from jax.experimental import pallas as pl
from jax.experimental.pallas import tpu as pltpu

EPS = 1e-6
N_MIXERS = 2
GRID_W = 64
GLA_HEADS = 4
GATE_TAU = 16.0
CHUNK = 64
LOG2_E = 1.4426950408889634

V7X_LANES = 128
V7X_SUBLANES = 8
V7X_VMEM_BYTES = 64 * 1024 * 1024
VMEM_LIMIT = V7X_VMEM_BYTES * 7 // 8

F32 = jnp.float32
BF16 = jnp.bfloat16

SH1, SC1, G1, SH2, SC2, G2 = range(6)


def _params(*sem):
    return pltpu.CompilerParams(dimension_semantics=sem, vmem_limit_bytes=VMEM_LIMIT)


def _layer_spec(param):
    arr, layer = param
    nd = arr.ndim - 1
    return pl.BlockSpec((None,) + arr.shape[1:], lambda *_: (layer,) + (0,) * nd, pipeline_mode=pl.Buffered(1))


def _specs_and_arrays(params):
    return [_layer_spec(p) for p in params], [p[0] for p in params]


def _rms(x, g):
    return x * lax.rsqrt(jnp.mean(x * x, axis=-1, keepdims=True) + EPS) * g


def _silu(x):
    return x * jax.nn.sigmoid(x)


def _norm_mod(x, g, mod_ref, shift_row, scale_row):
    h = _rms(x, g)
    return h * (1.0 + mod_ref[0, scale_row:scale_row + 1, :]) + mod_ref[0, shift_row:shift_row + 1, :]


def _residual(x, y, gpost, mod_ref, gate_row):
    return x + mod_ref[0, gate_row:gate_row + 1, :] * _rms(y, gpost)


def _dot(a, b):
    return jnp.dot(a, b, preferred_element_type=F32)


def _dot_nt(a, b):
    return lax.dot_general(a, b, (((1,), (1,)), ((), ())), preferred_element_type=F32)


def _dot_tn(a, b):
    return lax.dot_general(a, b, (((0,), (0,)), ((), ())), preferred_element_type=F32)


def _ada_kernel(cond_ref, w_ref, b_ref, o_ref):
    s = _silu(cond_ref[...])
    o_ref[0] = jnp.dot(s, w_ref[0], preferred_element_type=F32, precision=lax.Precision.HIGHEST) + b_ref[0]


def _ada_call(cond, ada_w, ada_b):
    depth, d, d6 = ada_w.shape
    rows = cond.shape[0]
    tn = d6 // 4
    return pl.pallas_call(
        _ada_kernel,
        grid=(depth, d6 // tn),
        in_specs=[
            pl.BlockSpec((rows, d), lambda i, j: (0, 0)),
            pl.BlockSpec((1, d, tn), lambda i, j: (i, 0, j)),
            pl.BlockSpec((1, 1, tn), lambda i, j: (i, 0, j)),
        ],
        out_specs=pl.BlockSpec((1, rows, tn), lambda i, j: (i, 0, j)),
        out_shape=jax.ShapeDtypeStruct((depth, rows, d6), F32),
        compiler_params=_params("parallel", "parallel"),
        name="ada",
    )(cond, ada_w, ada_b.reshape(depth, 1, d6))


CF_HALO = 16
CF_ROWS = 256
CF_LANES = 2 * V7X_LANES


def _cf_kernel(x_ref, xp_ref, xn_ref, mod_ref, gpre_ref, gpost_ref, w1_ref, b1_ref, dw_ref, dwb_ref, lng_ref,
               lnb_ref, w2_ref, b2_ref, o_ref, buf_ref, u_ref, *, width):
    t = pl.program_id(1)
    nt = pl.num_programs(1)
    tn, d = x_ref.shape[1:]
    pad = width // 2
    x = x_ref[0]
    gpre = gpre_ref[...]
    hb_ext = jnp.concatenate([_norm_mod(v, gpre, mod_ref, SH1, SC1).astype(BF16)
                              for v in (xp_ref[0], x, xn_ref[0])], axis=0)

    def stage_glu(cs):
        gate_cs = slice(d + cs.start, d + cs.stop)
        a = _dot(hb_ext, w1_ref[:, cs]) + b1_ref[:, cs]
        gt = _dot(hb_ext, w1_ref[:, gate_cs]) + b1_ref[:, gate_cs]
        glu = a * jax.nn.sigmoid(gt)
        buf_ref[0:CF_HALO, cs] = jnp.where(t > 0, glu[0:CF_HALO], 0.0)
        buf_ref[CF_HALO:CF_HALO + tn, cs] = glu[CF_HALO:CF_HALO + tn]
        buf_ref[CF_HALO + tn:, cs] = jnp.where(t < nt - 1, glu[CF_HALO + tn:], 0.0)

    lead = CF_HALO - pad
    groups = -(-(lead + width) // V7X_SUBLANES)
    strips = [slice(c0, c0 + CF_LANES) for c0 in range(0, d, CF_LANES)]
    stage_glu(strips[0])
    for si, cs in enumerate(strips):
        if si + 1 < len(strips):
            stage_glu(strips[si + 1])
        for l0 in range(cs.start, cs.stop, V7X_LANES):
            ls = slice(l0, l0 + V7X_LANES)
            for r0 in range(0, tn, CF_ROWS):
                acc = None
                for r in range(V7X_SUBLANES):
                    aligned = None
                    for a in range(groups):
                        k = V7X_SUBLANES * a + r - lead
                        if 0 <= k < width:
                            lo = r0 + V7X_SUBLANES * a
                            term = buf_ref[lo:lo + CF_ROWS + V7X_SUBLANES, ls] * dw_ref[k:k + 1, ls]
                            aligned = term if aligned is None else aligned + term
                    piece = aligned[r:r + CF_ROWS]
                    acc = piece if acc is None else acc + piece
                u_ref[r0:r0 + CF_ROWS, ls] = acc
    u = u_ref[...] + dwb_ref[...]
    mu = jnp.mean(u, axis=-1, keepdims=True)
    uc = u - mu
    var = jnp.mean(uc * uc, axis=-1, keepdims=True)
    v = _silu(uc * lax.rsqrt(var + EPS) * lng_ref[...] + lnb_ref[...])
    y = _dot(v.astype(BF16), w2_ref[...]) + b2_ref[...]
    o_ref[0] = _residual(x, y, gpost_ref[...], mod_ref, G1)


def _cf_call(x, mods, gpre, gpost, w1, b1, dw, dwb, lng, lnb, w2, b2, tn):
    bsz, n, d = x.shape
    width = dw[0].shape[1]
    assert width // 2 < CF_HALO and tn % CF_ROWS == 0 and d % CF_LANES == 0
    hb = tn // CF_HALO
    last = n // CF_HALO - 1
    tile = pl.BlockSpec((1, tn, d), lambda b, t: (b, t, 0))
    param_specs, param_arrays = _specs_and_arrays([gpre, gpost, w1, b1, dw, dwb, lng, lnb, w2, b2])
    return pl.pallas_call(
        functools.partial(_cf_kernel, width=width),
        grid=(bsz, n // tn),
        in_specs=[
            tile,
            pl.BlockSpec((1, CF_HALO, d), lambda b, t: (b, jnp.maximum(t * hb - 1, 0), 0)),
            pl.BlockSpec((1, CF_HALO, d), lambda b, t: (b, jnp.minimum((t + 1) * hb, last), 0)),
            pl.BlockSpec((1, 6, d), lambda b, t: (b, 0, 0)),
        ] + param_specs,
        out_specs=tile,
        out_shape=jax.ShapeDtypeStruct((bsz, n, d), F32),
        scratch_shapes=[pltpu.VMEM((tn + 2 * CF_HALO, d), F32), pltpu.VMEM((tn, d), F32)],
        compiler_params=_params("parallel", "parallel"),
        name="cf",
    )(x, x, x, mods, *param_arrays)


FFN_LANES = 8 * V7X_LANES


def _ffn_strips(f):
    return [slice(f0, min(f0 + FFN_LANES, f)) for f0 in range(0, f, FFN_LANES)]


def _ffn_kernel(*refs, gw, vertical):
    if vertical:
        x_ref, xp_ref, xn_ref = refs[:3]
        refs = refs[3:]
    else:
        x_ref = refs[0]
        refs = refs[1:]
    dw_ref, dwb_ref, wa_ref, wb_ref, wo_ref, mod_ref, gpre_ref, gpost_ref, o_ref, gated_ref = refs
    t = pl.program_id(1)
    nt = pl.num_programs(1)
    tn = x_ref.shape[1]
    f = wa_ref.shape[1]
    nrows = tn // gw
    x = x_ref[0]
    gpre = gpre_ref[...]
    hb = _norm_mod(x, gpre, mod_ref, SH2, SC2).astype(BF16)
    if vertical:
        before = jnp.where(t > 0, _norm_mod(xp_ref[0], gpre, mod_ref, SH2, SC2), 0.0).astype(BF16)
        after = jnp.where(t < nt - 1, _norm_mod(xn_ref[0], gpre, mod_ref, SH2, SC2), 0.0).astype(BF16)
        hb_ext = jnp.concatenate([before, hb, after], axis=0)
    else:
        hb_ext = hb
    y = None
    strips = _ffn_strips(f)

    a_next, gate_next = _dot(hb_ext, wa_ref[:, strips[0]]), _dot(hb, wb_ref[:, strips[0]])
    for si, fs in enumerate(strips):
        fl = fs.stop - fs.start
        a_ext, gate = a_next, gate_next
        sub = lax.broadcasted_iota(jnp.int32, (V7X_SUBLANES, fl), 0)
        w = [dw_ref[k:k + 1, fs] for k in range(dw_ref.shape[0])]
        bias = dwb_ref[:, fs]
        if si + 1 < len(strips):
            a_next = _dot(hb_ext, wa_ref[:, strips[si + 1]])
            gate_next = _dot(hb, wb_ref[:, strips[si + 1]])
        if si > 0:
            part = _dot(gated_ref[:, strips[si - 1]], wo_ref[strips[si - 1], :])
            y = part if y is None else y + part
        for i in range(nrows):
            rs = slice(i * gw, (i + 1) * gw)
            if vertical:
                up, mid, dn = [a_ext[(i + j) * gw:(i + j + 1) * gw] for j in range(3)]
                left, centre, right = [w[j] * up + w[3 + j] * mid + w[6 + j] * dn for j in range(3)]
            else:
                mid = a_ext[rs]
                left, centre, right = [w[3 + j] * mid for j in range(3)]
            from_left = pltpu.roll(left, 1, 0)
            from_left = jnp.concatenate([jnp.where(sub > 0, from_left[:V7X_SUBLANES], 0.0),
                                         from_left[V7X_SUBLANES:]], axis=0)
            from_right = pltpu.roll(right, gw - 1, 0)
            from_right = jnp.concatenate([from_right[:gw - V7X_SUBLANES],
                                          jnp.where(sub < V7X_SUBLANES - 1, from_right[gw - V7X_SUBLANES:], 0.0)],
                                         axis=0)
            conv = centre + bias + from_left + from_right
            gated_ref[rs, fs] = (_silu(conv) * gate[rs]).astype(BF16)
    part = _dot(gated_ref[:, strips[-1]], wo_ref[strips[-1], :])
    y = part if y is None else y + part
    o_ref[0] = _residual(x, y, gpost_ref[...], mod_ref, G2)


def _ffn_call(x, mods, dw, dwb, wa, wb, wo, gpre, gpost, tn, gw, vertical):
    bsz, n, d = x.shape
    f = wa[0].shape[-1]
    assert tn % gw == 0
    hb = tn // gw
    last = n // gw - 1
    tile_d = pl.BlockSpec((1, tn, d), lambda b, t: (b, t, 0))
    in_specs = [tile_d]
    args = [x]
    if vertical:
        in_specs += [pl.BlockSpec((1, gw, d), lambda b, t: (b, jnp.maximum(t * hb - 1, 0), 0)),
                     pl.BlockSpec((1, gw, d), lambda b, t: (b, jnp.minimum((t + 1) * hb, last), 0))]
        args += [x, x]
    param_specs, param_arrays = _specs_and_arrays([dw, dwb, wa, wb, wo])
    norm_specs, norm_arrays = _specs_and_arrays([gpre, gpost])
    in_specs += param_specs + [pl.BlockSpec((1, 6, d), lambda b, t: (b, 0, 0))] + norm_specs
    args += param_arrays + [mods] + norm_arrays
    return pl.pallas_call(
        functools.partial(_ffn_kernel, gw=gw, vertical=vertical),
        grid=(bsz, n // tn),
        in_specs=in_specs,
        out_specs=tile_d,
        out_shape=jax.ShapeDtypeStruct((bsz, n, d), F32),
        scratch_shapes=[pltpu.VMEM((tn, f), BF16)],
        compiler_params=_params("parallel", "parallel"),
        name="ffn_grid" if vertical else "ffn_seq",
    )(*args)


def _log_sigmoid(z):
    return jnp.minimum(z, 0.0) - jnp.log(1.0 + jnp.exp(-jnp.abs(z)))


def _gla_proj_kernel(x_ref, mod_ref, g_ref, wq_ref, wk_ref, wv_ref, wr_ref, wg1_ref, wg2_ref, bg_ref,
                     q_ref, k_ref, v_ref, r_ref, cf_ref, cb_ref, *, q_scale):
    tn = x_ref.shape[1]
    dk = q_ref.shape[-1]
    hb = _norm_mod(x_ref[0], g_ref[...], mod_ref, SH1, SC1).astype(BF16)
    low = _dot(hb, wg1_ref[...]).astype(BF16)
    z = _dot(low, wg2_ref[...]) + bg_ref[...]
    q_ref[0] = _dot(hb, wq_ref[...]) * q_scale
    k_ref[0] = _dot(hb, wk_ref[...])
    g = _log_sigmoid(z) * (LOG2_E / GATE_TAU)
    v_ref[0] = _dot(hb, wv_ref[...])
    r_ref[0] = _silu(_dot(hb, wr_ref[...])).astype(r_ref.dtype)
    g_hi = g.astype(BF16)
    rest = g - g_hi.astype(F32)
    g_mid = rest.astype(BF16)
    g_lo = (rest - g_mid.astype(F32)).astype(BF16)
    row = lax.broadcasted_iota(jnp.int32, (CHUNK, 3 * CHUNK), 0)
    colm = lax.broadcasted_iota(jnp.int32, (CHUNK, 3 * CHUNK), 1) & (CHUNK - 1)
    tri_f = jnp.where(colm <= row, 1.0, 0.0).astype(BF16)
    tri_b = jnp.where(colm >= row, 1.0, 0.0).astype(BF16)
    for c0 in range(0, tn, CHUNK):
        cs = slice(c0, c0 + CHUNK)
        pieces = jnp.concatenate([g_hi[cs], g_mid[cs], g_lo[cs]], axis=0)
        cf_ref[0, cs, :] = _dot(tri_f, pieces[:, :dk])
        cb_ref[0, cs, :] = _dot(tri_b, pieces[:, dk:])


def _gla_proj_call(x, mods, g, wq, wk, wv, wr, wg1, wg2, bg, tn):
    bsz, n, d = x.shape
    dk, dv = wq[0].shape[-1], wv[0].shape[-1]
    tile = lambda w: pl.BlockSpec((1, tn, w), lambda b, t: (b, t, 0))
    shp = lambda w: jax.ShapeDtypeStruct((bsz, n, w), F32)
    param_specs, param_arrays = _specs_and_arrays([g, wq, wk, wv, wr, wg1, wg2, bg])
    return pl.pallas_call(
        functools.partial(_gla_proj_kernel, q_scale=float((dk // GLA_HEADS) ** -0.5)),
        grid=(bsz, n // tn),
        in_specs=[tile(d), pl.BlockSpec((1, 6, d), lambda b, t: (b, 0, 0))] + param_specs,
        out_specs=[tile(dk), tile(dk), tile(dv), tile(dv), tile(dk), tile(dk)],
        out_shape=[shp(dk), shp(dk), shp(dv), jax.ShapeDtypeStruct((bsz, n, dv), BF16), shp(dk), shp(dk)],
        compiler_params=_params("parallel", "parallel"),
        name="gla_proj",
    )(x, mods, *param_arrays)


CHUNK_LEVELS = 6
assert 1 << CHUNK_LEVELS == CHUNK


def _level_reference(c_ref, base, level, reverse):
    half = 1 << level
    blk = 2 * half
    off = half if reverse else half - 1
    dk = c_ref.shape[-1]

    def row(r):
        return c_ref[0, pl.ds(base + (r + off), 1), :]

    if blk >= V7X_SUBLANES:
        return jnp.concatenate([jnp.broadcast_to(row(b0), (blk, dk)) for b0 in range(0, CHUNK, blk)], axis=0)
    sub = lax.broadcasted_iota(jnp.int32, (V7X_SUBLANES, dk), 0)
    groups = []
    for g0 in range(0, CHUNK, V7X_SUBLANES):
        r = jnp.broadcast_to(row(g0), (V7X_SUBLANES, dk))
        for b0 in range(blk, V7X_SUBLANES, blk):
            r = jnp.where(sub >= b0, jnp.broadcast_to(row(g0 + b0), (V7X_SUBLANES, dk)), r)
        groups.append(r)
    return jnp.concatenate(groups, axis=0)


def _gla_scan_kernel(q_ref, k_ref, v_ref, cf_ref, cb_ref, sf0_ref, sb0_ref, o_ref, sff_ref, sbf_ref,
                     stf_ref, stb_ref, incf_ref, incb_ref, *, nchunks):
    stf_ref[...] = sf0_ref[0, 0]
    stb_ref[...] = sb0_ref[0, 0]
    dk = q_ref.shape[-1]
    ti = lax.broadcasted_iota(jnp.int32, (CHUNK, CHUNK), 0)
    si = lax.broadcasted_iota(jnp.int32, (CHUNK, CHUNK), 1)
    x = ti ^ si
    lvl = sum([(x >= (1 << b)).astype(jnp.int32) for b in range(CHUNK_LEVELS)]) - 1
    row = lax.broadcasted_iota(jnp.int32, (CHUNK, dk), 0)
    below_split = [((row >> level) & 1) == 1 for level in range(CHUNK_LEVELS)]

    def chunk_scores(base):
        rows = pl.ds(base, CHUNK)
        q, k, cf, cb = q_ref[0, rows, :], k_ref[0, rows, :], cf_ref[0, rows, :], cb_ref[0, rows, :]
        kb = k.astype(BF16)
        scores = None
        for level in range(CHUNK_LEVELS):
            half = 1 << level
            if level == 0:
                xq = jnp.where(below_split[0], cf - pltpu.roll(cf, 1, 0), cb - pltpu.roll(cb, CHUNK - 1, 0))
                part = _dot_nt((q * jnp.exp2(xq)).astype(BF16), kb)
                scores = jnp.where(lvl == 0, part, 0.0)
                continue
            if half >= V7X_SUBLANES:
                xq, xk = [], []
                for b0 in range(0, CHUNK, 2 * half):
                    above, below = slice(b0, b0 + half), slice(b0 + half, b0 + 2 * half)
                    rf = cf_ref[0, pl.ds(base + (b0 + half - 1), 1), :]
                    rb = cb_ref[0, pl.ds(base + (b0 + half), 1), :]
                    xq += [cb[above] - rb, cf[below] - rf]
                    xk += [rf - cf[above], rb - cb[below]]
                xq, xk = jnp.concatenate(xq, axis=0), jnp.concatenate(xk, axis=0)
            else:
                df = cf - _level_reference(cf_ref, base, level, False)
                db = cb - _level_reference(cb_ref, base, level, True)
                xq = jnp.where(below_split[level], df, db)
                xk = -jnp.where(below_split[level], db, df)
            part = _dot_nt((q * jnp.exp2(xq)).astype(BF16), (k * jnp.exp2(xk)).astype(BF16))
            scores = jnp.where(lvl == level, part, scores)
        return scores.astype(BF16)

    def state_increment(base, c_ref, end_row):
        rows = pl.ds(base, CHUNK)
        c_end = c_ref[0, pl.ds(base + end_row, 1), :]
        kd = (k_ref[0, rows, :] * jnp.exp2(c_end - c_ref[0, rows, :])).astype(BF16)
        return _dot_tn(kd, v_ref[0, rows, :].astype(BF16))

    def carried(base, c_ref, end_row, st_ref, inc):
        rows = pl.ds(base, CHUNK)
        st = st_ref[...]
        o = _dot((q_ref[0, rows, :] * jnp.exp2(c_ref[0, rows, :])).astype(BF16), st.astype(BF16))
        c_end = c_ref[0, pl.ds(base + end_row, 1), :]
        decay = jnp.broadcast_to(jnp.exp2(c_end), (dk, dk)).T[:, 0:1]
        st_ref[...] = st * decay + inc
        return o

    def chunk_base(ci):
        return pl.multiple_of(ci * CHUNK, CHUNK)

    def produce(i, slot):
        fbase = chunk_base(jnp.minimum(i, nchunks - 1))
        bbase = chunk_base(jnp.maximum(nchunks - 1 - i, 0))
        incf_ref[slot] = state_increment(fbase, cf_ref, CHUNK - 1)
        incb_ref[slot] = state_increment(bbase, cb_ref, 0)
        return chunk_scores(fbase)

    def step(i, scores, first, slot):
        next_scores = produce(i + 1, 1 - slot)
        fbase, bbase = chunk_base(i), chunk_base(nchunks - 1 - i)
        frows, brows = pl.ds(fbase, CHUNK), pl.ds(bbase, CHUNK)
        q, k, v = q_ref[0, frows, :], k_ref[0, frows, :], v_ref[0, frows, :]
        diag = jnp.sum(q * k, axis=-1, keepdims=True)
        o_f = (_dot(scores, v.astype(BF16)) + (2.0 * diag) * v
               + carried(fbase, cf_ref, CHUNK - 1, stf_ref, incf_ref[slot]))
        o_b = carried(bbase, cb_ref, 0, stb_ref, incb_ref[slot])
        for rows, part in ((frows, o_f), (brows, o_b)):
            if not first:
                part = part + o_ref[0, rows, :].astype(F32)
            o_ref[0, rows, :] = part.astype(o_ref.dtype)
        return next_scores

    half = nchunks // 2
    per_trip = 4 if half % 4 == 0 else 2

    def trip(first):
        def body(j, s):
            for u in range(per_trip):
                s = step(per_trip * j + u, s, first, u % 2)
            return s
        return body

    scores = produce(0, 0)
    scores = lax.fori_loop(0, half // per_trip, trip(True), scores)
    lax.fori_loop(half // per_trip, nchunks // per_trip, trip(False), scores)
    sff_ref[0, 0] = stf_ref[...]
    sbf_ref[0, 0] = stb_ref[...]


def _gla_scan_call(q, k, v, cf, cb, s_f, s_b):
    bsz, n, dk = q.shape
    dv = v.shape[-1]
    hk, hv = dk // GLA_HEADS, dv // GLA_HEADS
    nchunks = n // CHUNK
    assert nchunks % 4 == 0
    tile = lambda w: pl.BlockSpec((1, n, w), lambda b, h: (b, 0, h))
    state = pl.BlockSpec((1, 1, hk, hv), lambda b, h: (b, h, 0, 0))
    state_shape = jax.ShapeDtypeStruct((bsz, GLA_HEADS, hk, hv), F32)
    return pl.pallas_call(
        functools.partial(_gla_scan_kernel, nchunks=nchunks),
        grid=(bsz, GLA_HEADS),
        in_specs=[tile(hk), tile(hk), tile(hv), tile(hk), tile(hk), state, state],
        out_specs=[tile(hv), state, state],
        out_shape=[jax.ShapeDtypeStruct((bsz, n, dv), BF16), state_shape, state_shape],
        scratch_shapes=[pltpu.VMEM((hk, hv), F32), pltpu.VMEM((hk, hv), F32),
                        pltpu.VMEM((2, hk, hv), F32), pltpu.VMEM((2, hk, hv), F32)],
        compiler_params=_params("parallel", "parallel"),
        name="gla_scan",
    )(q, k, v, cf, cb, s_f, s_b)


def _gla_out_kernel(o_in_ref, r_ref, ng_ref, wo_ref, x_ref, mod_ref, gpost_ref, o_ref, *, heads):
    o = o_in_ref[0].astype(F32)
    hv = o.shape[-1] // heads
    parts = []
    for h in range(heads):
        oh = o[:, h * hv:(h + 1) * hv]
        parts.append(oh * lax.rsqrt(jnp.mean(oh * oh, axis=-1, keepdims=True) + EPS))
    gated = jnp.concatenate(parts, axis=-1) * ng_ref[...] * r_ref[0].astype(F32)
    y = _dot(gated.astype(BF16), wo_ref[...])
    o_ref[0] = _residual(x_ref[0], y, gpost_ref[...], mod_ref, G1)


def _gla_out_call(o, r, ng, wo, x, mods, gpost, tn):
    bsz, n, d = x.shape
    dv = o.shape[-1]
    tile_v = pl.BlockSpec((1, tn, dv), lambda b, t: (b, t, 0))
    tile_d = pl.BlockSpec((1, tn, d), lambda b, t: (b, t, 0))
    return pl.pallas_call(
        functools.partial(_gla_out_kernel, heads=GLA_HEADS),
        grid=(bsz, n // tn),
        in_specs=[tile_v, tile_v, _layer_spec(ng), _layer_spec(wo), tile_d,
                  pl.BlockSpec((1, 6, d), lambda b, t: (b, 0, 0)), _layer_spec(gpost)],
        out_specs=tile_d,
        out_shape=jax.ShapeDtypeStruct((bsz, n, d), F32),
        compiler_params=_params("parallel", "parallel"),
        name="gla_out",
    )(o, r, ng[0], wo[0], x, mods, gpost[0])


def _tile(n, want):
    return min(n, want)


def _conformer(x, mods, g_pre, g_post, w):
    return _cf_call(x, mods, g_pre, g_post, w["w1"], w["b1"], w["dw"], w["dwb"], w["ln_g"], w["ln_b"], w["w2"],
                    w["b2"], _tile(x.shape[1], 512))


def _gla_states(h, mods, g_pre, w, s_f, s_b):
    n = h.shape[1]
    q, k, v, r, cf, cb = _gla_proj_call(h, mods, g_pre, w["wq"], w["wk"], w["wv"], w["wr"], w["wg1"], w["wg2"],
                                        w["bg"], _tile(n, 512))
    o, fin_f, fin_b = _gla_scan_call(q, k, v, cf, cb, s_f, s_b)
    return o, r, fin_f, fin_b


def _gla_readout(x, mods, g_post, w, o, r):
    return _gla_out_call(o, r, w["norm_g"], w["wo"], x, mods, g_post, _tile(x.shape[1], 1024))


def _conv_ffn(x, mods, g_pre, g_post, w, on_grid):
    n = x.shape[1]
    if on_grid:
        return _ffn_call(x, mods, w["dw"], w["dwb"], w["wa"], w["wb"], w["wo"], g_pre, g_post, _tile(n, 512),
                         GRID_W, True)
    return _ffn_call(x, mods, w["dw"], w["dwb"], w["wa"], w["wb"], w["wo"], g_pre, g_post, n, n, False)


def kernel(x, c, ctx, c_ctx, ada_w, ada_b, norm_pre_mix, norm_post_mix, norm_pre_ffn, norm_post_ffn, cf_w1, cf_b1, cf_dw, cf_dwb, cf_ln_g, cf_ln_b, cf_w2, cf_b2, gla_wq, gla_wk, gla_wv, gla_wr, gla_wg1, gla_wg2, gla_bg, gla_norm_g, gla_wo, ffn_wa, ffn_wb, ffn_dw, ffn_dwb, ffn_wo):
    bsz, n, d = x.shape
    depth = ada_w.shape[0]
    dk = gla_wq.shape[-1]
    dv = gla_wv.shape[-1]
    hk, hv = dk // GLA_HEADS, dv // GLA_HEADS
    rank = gla_wg1.shape[-1]
    rows = lambda v: v.reshape(v.shape[0], 1, -1)

    cond_rows = -(-(bsz + 1) // V7X_SUBLANES) * V7X_SUBLANES
    cond = jnp.zeros((cond_rows, d), F32).at[:bsz].set(c).at[bsz].set(c_ctx)
    mods = _ada_call(cond, ada_w, ada_b)

    pre_mix, post_mix, pre_ffn, post_ffn = rows(norm_pre_mix), rows(norm_post_mix), rows(norm_pre_ffn), rows(norm_post_ffn)
    cf = dict(w1=cf_w1.astype(BF16), b1=rows(cf_b1), dw=cf_dw, dwb=rows(cf_dwb), ln_g=rows(cf_ln_g),
              ln_b=rows(cf_ln_b), w2=cf_w2.astype(BF16), b2=rows(cf_b2))
    wg2 = jnp.zeros((gla_wg2.shape[0], 2 * rank, 2 * dk), F32)
    wg2 = wg2.at[:, :rank, :dk].set(gla_wg2[:, 0]).at[:, rank:, dk:].set(gla_wg2[:, 1])
    gla = dict(wq=gla_wq.astype(BF16), wk=gla_wk.astype(BF16), wv=gla_wv.astype(BF16), wr=gla_wr.astype(BF16),
               wg1=jnp.concatenate([gla_wg1[:, 0], gla_wg1[:, 1]], axis=-1).astype(BF16), wg2=wg2.astype(BF16),
               bg=gla_bg.reshape(gla_bg.shape[0], 1, 2 * dk), norm_g=rows(jnp.tile(gla_norm_g, (1, GLA_HEADS))),
               wo=gla_wo.astype(BF16))
    ffn = dict(wa=ffn_wa.astype(BF16), wb=ffn_wb.astype(BF16),
               dw=ffn_dw.reshape(depth, -1, ffn_dw.shape[-1]), dwb=rows(ffn_dwb), wo=ffn_wo.astype(BF16))
    layer = lambda params, idx: {name: (arr, idx) for name, arr in params.items()}

    for i in range(depth):
        last = i == depth - 1
        j = i // N_MIXERS
        mx = mods[i, :bsz].reshape(bsz, 6, d)
        mc = jnp.broadcast_to(mods[i, bsz].reshape(1, 6, d), (bsz, 6, d))
        g_pre, g_post = (pre_mix, i), (post_mix, i)
        if i % N_MIXERS == 0:
            w = layer(cf, j)
            x = _conformer(x, mx, g_pre, g_post, w)
            if not last:
                ctx_mixed = _conformer(ctx, mc, g_pre, g_post, w)
        else:
            w = layer(gla, j)
            zeros = jnp.zeros((bsz, GLA_HEADS, hk, hv), F32)
            oc, rc, s_f, s_b = _gla_states(ctx, mc, g_pre, w, zeros, zeros)
            if not last:
                ctx_mixed = _gla_readout(ctx, mc, g_post, w, oc, rc)
            ox, rx, _, _ = _gla_states(x, mx, g_pre, w, s_f, s_b)
            x = _gla_readout(x, mx, g_post, w, ox, rx)
        fw = layer(ffn, i)
        g_pre_f, g_post_f = (pre_ffn, i), (post_ffn, i)
        x = _conv_ffn(x, mx, g_pre_f, g_post_f, fw, True)
        if not last:
            ctx = _conv_ffn(ctx_mixed, mc, g_pre_f, g_post_f, fw, False)
    return x
```

```python
import functools

import jax
import jax.numpy as jnp
from jax import lax
from jax.experimental import pallas as pl
from jax.experimental.pallas import tpu as pltpu

EPS = 1e-6
N_MIXERS = 2
GRID_W = 64
GLA_HEADS = 4
GATE_TAU = 16.0
CHUNK = 64
LOG2_E = 1.4426950408889634

V7X_LANES = 128
V7X_SUBLANES = 8
V7X_VMEM_BYTES = 64 * 1024 * 1024
VMEM_LIMIT = V7X_VMEM_BYTES * 7 // 8

F32 = jnp.float32
BF16 = jnp.bfloat16

SH1, SC1, G1, SH2, SC2, G2 = range(6)


def _params(*sem):
    return pltpu.CompilerParams(dimension_semantics=sem, vmem_limit_bytes=VMEM_LIMIT)


def _layer_spec(param):
    arr, layer = param
    nd = arr.ndim - 1
    return pl.BlockSpec((None,) + arr.shape[1:], lambda *_: (layer,) + (0,) * nd, pipeline_mode=pl.Buffered(1))


def _specs_and_arrays(params):
    return [_layer_spec(p) for p in params], [p[0] for p in params]


def _rms(x, g):
    return x * lax.rsqrt(jnp.mean(x * x, axis=-1, keepdims=True) + EPS) * g


def _silu(x):
    return x * jax.nn.sigmoid(x)


def _norm_mod(x, g, mod_ref, shift_row, scale_row):
    h = _rms(x, g)
    return h * (1.0 + mod_ref[0, scale_row:scale_row + 1, :]) + mod_ref[0, shift_row:shift_row + 1, :]


def _residual(x, y, gpost, mod_ref, gate_row):
    return x + mod_ref[0, gate_row:gate_row + 1, :] * _rms(y, gpost)


def _dot(a, b):
    return jnp.dot(a, b, preferred_element_type=F32)


def _dot_nt(a, b):
    return lax.dot_general(a, b, (((1,), (1,)), ((), ())), preferred_element_type=F32)


def _dot_tn(a, b):
    return lax.dot_general(a, b, (((0,), (0,)), ((), ())), preferred_element_type=F32)


def _ada_kernel(cond_ref, w_ref, b_ref, o_ref):
    s = _silu(cond_ref[...])
    o_ref[0] = jnp.dot(s, w_ref[0], preferred_element_type=F32, precision=lax.Precision.HIGHEST) + b_ref[0]


def _ada_call(cond, ada_w, ada_b):
    depth, d, d6 = ada_w.shape
    rows = cond.shape[0]
    tn = d6 // 4
    return pl.pallas_call(
        _ada_kernel,
        grid=(depth, d6 // tn),
        in_specs=[
            pl.BlockSpec((rows, d), lambda i, j: (0, 0)),
            pl.BlockSpec((1, d, tn), lambda i, j: (i, 0, j)),
            pl.BlockSpec((1, 1, tn), lambda i, j: (i, 0, j)),
        ],
        out_specs=pl.BlockSpec((1, rows, tn), lambda i, j: (i, 0, j)),
        out_shape=jax.ShapeDtypeStruct((depth, rows, d6), F32),
        compiler_params=_params("parallel", "parallel"),
        name="ada",
    )(cond, ada_w, ada_b.reshape(depth, 1, d6))


CF_HALO = 16
CF_ROWS = 256
CF_LANES = 2 * V7X_LANES


def _cf_kernel(x_ref, xp_ref, xn_ref, mod_ref, gpre_ref, gpost_ref, w1_ref, b1_ref, dw_ref, dwb_ref, lng_ref,
               lnb_ref, w2_ref, b2_ref, o_ref, buf_ref, u_ref, *, width):
    t = pl.program_id(1)
    nt = pl.num_programs(1)
    tn, d = x_ref.shape[1:]
    pad = width // 2
    x = x_ref[0]
    gpre = gpre_ref[...]
    hb_ext = jnp.concatenate([_norm_mod(v, gpre, mod_ref, SH1, SC1).astype(BF16)
                              for v in (xp_ref[0], x, xn_ref[0])], axis=0)

    def stage_glu(cs):
        gate_cs = slice(d + cs.start, d + cs.stop)
        a = _dot(hb_ext, w1_ref[:, cs]) + b1_ref[:, cs]
        gt = _dot(hb_ext, w1_ref[:, gate_cs]) + b1_ref[:, gate_cs]
        glu = a * jax.nn.sigmoid(gt)
        buf_ref[0:CF_HALO, cs] = jnp.where(t > 0, glu[0:CF_HALO], 0.0)
        buf_ref[CF_HALO:CF_HALO + tn, cs] = glu[CF_HALO:CF_HALO + tn]
        buf_ref[CF_HALO + tn:, cs] = jnp.where(t < nt - 1, glu[CF_HALO + tn:], 0.0)

    lead = CF_HALO - pad
    groups = -(-(lead + width) // V7X_SUBLANES)
    strips = [slice(c0, c0 + CF_LANES) for c0 in range(0, d, CF_LANES)]
    stage_glu(strips[0])
    for si, cs in enumerate(strips):
        if si + 1 < len(strips):
            stage_glu(strips[si + 1])
        for l0 in range(cs.start, cs.stop, V7X_LANES):
            ls = slice(l0, l0 + V7X_LANES)
            for r0 in range(0, tn, CF_ROWS):
                acc = None
                for r in range(V7X_SUBLANES):
                    aligned = None
                    for a in range(groups):
                        k = V7X_SUBLANES * a + r - lead
                        if 0 <= k < width:
                            lo = r0 + V7X_SUBLANES * a
                            term = buf_ref[lo:lo + CF_ROWS + V7X_SUBLANES, ls] * dw_ref[k:k + 1, ls]
                            aligned = term if aligned is None else aligned + term
                    piece = aligned[r:r + CF_ROWS]
                    acc = piece if acc is None else acc + piece
                u_ref[r0:r0 + CF_ROWS, ls] = acc
    u = u_ref[...] + dwb_ref[...]
    mu = jnp.mean(u, axis=-1, keepdims=True)
    uc = u - mu
    var = jnp.mean(uc * uc, axis=-1, keepdims=True)
    v = _silu(uc * lax.rsqrt(var + EPS) * lng_ref[...] + lnb_ref[...])
    y = _dot(v.astype(BF16), w2_ref[...]) + b2_ref[...]
    o_ref[0] = _residual(x, y, gpost_ref[...], mod_ref, G1)


def _cf_call(x, mods, gpre, gpost, w1, b1, dw, dwb, lng, lnb, w2, b2, tn):
    bsz, n, d = x.shape
    width = dw[0].shape[1]
    assert width // 2 < CF_HALO and tn % CF_ROWS == 0 and d % CF_LANES == 0
    hb = tn // CF_HALO
    last = n // CF_HALO - 1
    tile = pl.BlockSpec((1, tn, d), lambda b, t: (b, t, 0))
    param_specs, param_arrays = _specs_and_arrays([gpre, gpost, w1, b1, dw, dwb, lng, lnb, w2, b2])
    return pl.pallas_call(
        functools.partial(_cf_kernel, width=width),
        grid=(bsz, n // tn),
        in_specs=[
            tile,
            pl.BlockSpec((1, CF_HALO, d), lambda b, t: (b, jnp.maximum(t * hb - 1, 0), 0)),
            pl.BlockSpec((1, CF_HALO, d), lambda b, t: (b, jnp.minimum((t + 1) * hb, last), 0)),
            pl.BlockSpec((1, 6, d), lambda b, t: (b, 0, 0)),
        ] + param_specs,
        out_specs=tile,
        out_shape=jax.ShapeDtypeStruct((bsz, n, d), F32),
        scratch_shapes=[pltpu.VMEM((tn + 2 * CF_HALO, d), F32), pltpu.VMEM((tn, d), F32)],
        compiler_params=_params("parallel", "parallel"),
        name="cf",
    )(x, x, x, mods, *param_arrays)


FFN_LANES = 8 * V7X_LANES


def _ffn_strips(f):
    return [slice(f0, min(f0 + FFN_LANES, f)) for f0 in range(0, f, FFN_LANES)]


def _ffn_kernel(*refs, gw, vertical):
    if vertical:
        x_ref, xp_ref, xn_ref = refs[:3]
        refs = refs[3:]
    else:
        x_ref = refs[0]
        refs = refs[1:]
    dw_ref, dwb_ref, wa_ref, wb_ref, wo_ref, mod_ref, gpre_ref, gpost_ref, o_ref, gated_ref = refs
    t = pl.program_id(1)
    nt = pl.num_programs(1)
    tn = x_ref.shape[1]
    f = wa_ref.shape[1]
    nrows = tn // gw
    x = x_ref[0]
    gpre = gpre_ref[...]
    hb = _norm_mod(x, gpre, mod_ref, SH2, SC2).astype(BF16)
    if vertical:
        before = jnp.where(t > 0, _norm_mod(xp_ref[0], gpre, mod_ref, SH2, SC2), 0.0).astype(BF16)
        after = jnp.where(t < nt - 1, _norm_mod(xn_ref[0], gpre, mod_ref, SH2, SC2), 0.0).astype(BF16)
        hb_ext = jnp.concatenate([before, hb, after], axis=0)
    else:
        hb_ext = hb
    y = None
    strips = _ffn_strips(f)

    a_next, gate_next = _dot(hb_ext, wa_ref[:, strips[0]]), _dot(hb, wb_ref[:, strips[0]])
    for si, fs in enumerate(strips):
        fl = fs.stop - fs.start
        a_ext, gate = a_next, gate_next
        sub = lax.broadcasted_iota(jnp.int32, (V7X_SUBLANES, fl), 0)
        w = [dw_ref[k:k + 1, fs] for k in range(dw_ref.shape[0])]
        bias = dwb_ref[:, fs]
        if si + 1 < len(strips):
            a_next = _dot(hb_ext, wa_ref[:, strips[si + 1]])
            gate_next = _dot(hb, wb_ref[:, strips[si + 1]])
        if si > 0:
            part = _dot(gated_ref[:, strips[si - 1]], wo_ref[strips[si - 1], :])
            y = part if y is None else y + part
        for i in range(nrows):
            rs = slice(i * gw, (i + 1) * gw)
            if vertical:
                up, mid, dn = [a_ext[(i + j) * gw:(i + j + 1) * gw] for j in range(3)]
                left, centre, right = [w[j] * up + w[3 + j] * mid + w[6 + j] * dn for j in range(3)]
            else:
                mid = a_ext[rs]
                left, centre, right = [w[3 + j] * mid for j in range(3)]
            from_left = pltpu.roll(left, 1, 0)
            from_left = jnp.concatenate([jnp.where(sub > 0, from_left[:V7X_SUBLANES], 0.0),
                                         from_left[V7X_SUBLANES:]], axis=0)
            from_right = pltpu.roll(right, gw - 1, 0)
            from_right = jnp.concatenate([from_right[:gw - V7X_SUBLANES],
                                          jnp.where(sub < V7X_SUBLANES - 1, from_right[gw - V7X_SUBLANES:], 0.0)],
                                         axis=0)
            conv = centre + bias + from_left + from_right
            gated_ref[rs, fs] = (_silu(conv) * gate[rs]).astype(BF16)
    part = _dot(gated_ref[:, strips[-1]], wo_ref[strips[-1], :])
    y = part if y is None else y + part
    o_ref[0] = _residual(x, y, gpost_ref[...], mod_ref, G2)


def _ffn_call(x, mods, dw, dwb, wa, wb, wo, gpre, gpost, tn, gw, vertical):
    bsz, n, d = x.shape
    f = wa[0].shape[-1]
    assert tn % gw == 0
    hb = tn // gw
    last = n // gw - 1
    tile_d = pl.BlockSpec((1, tn, d), lambda b, t: (b, t, 0))
    in_specs = [tile_d]
    args = [x]
    if vertical:
        in_specs += [pl.BlockSpec((1, gw, d), lambda b, t: (b, jnp.maximum(t * hb - 1, 0), 0)),
                     pl.BlockSpec((1, gw, d), lambda b, t: (b, jnp.minimum((t + 1) * hb, last), 0))]
        args += [x, x]
    param_specs, param_arrays = _specs_and_arrays([dw, dwb, wa, wb, wo])
    norm_specs, norm_arrays = _specs_and_arrays([gpre, gpost])
    in_specs += param_specs + [pl.BlockSpec((1, 6, d), lambda b, t: (b, 0, 0))] + norm_specs
    args += param_arrays + [mods] + norm_arrays
    return pl.pallas_call(
        functools.partial(_ffn_kernel, gw=gw, vertical=vertical),
        grid=(bsz, n // tn),
        in_specs=in_specs,
        out_specs=tile_d,
        out_shape=jax.ShapeDtypeStruct((bsz, n, d), F32),
        scratch_shapes=[pltpu.VMEM((tn, f), BF16)],
        compiler_params=_params("parallel", "parallel"),
        name="ffn_grid" if vertical else "ffn_seq",
    )(*args)


def _log_sigmoid(z):
    return jnp.minimum(z, 0.0) - jnp.log(1.0 + jnp.exp(-jnp.abs(z)))


def _gla_proj_kernel(x_ref, mod_ref, g_ref, wq_ref, wk_ref, wv_ref, wr_ref, wg1_ref, wg2_ref, bg_ref,
                     q_ref, k_ref, v_ref, r_ref, cf_ref, cb_ref, *, q_scale):
    tn = x_ref.shape[1]
    dk = q_ref.shape[-1]
    hb = _norm_mod(x_ref[0], g_ref[...], mod_ref, SH1, SC1).astype(BF16)
    low = _dot(hb, wg1_ref[...]).astype(BF16)
    z = _dot(low, wg2_ref[...]) + bg_ref[...]
    q_ref[0] = _dot(hb, wq_ref[...]) * q_scale
    k_ref[0] = _dot(hb, wk_ref[...])
    g = _log_sigmoid(z) * (LOG2_E / GATE_TAU)
    v_ref[0] = _dot(hb, wv_ref[...])
    r_ref[0] = _silu(_dot(hb, wr_ref[...])).astype(r_ref.dtype)
    g_hi = g.astype(BF16)
    rest = g - g_hi.astype(F32)
    g_mid = rest.astype(BF16)
    g_lo = (rest - g_mid.astype(F32)).astype(BF16)
    row = lax.broadcasted_iota(jnp.int32, (CHUNK, 3 * CHUNK), 0)
    colm = lax.broadcasted_iota(jnp.int32, (CHUNK, 3 * CHUNK), 1) & (CHUNK - 1)
    tri_f = jnp.where(colm <= row, 1.0, 0.0).astype(BF16)
    tri_b = jnp.where(colm >= row, 1.0, 0.0).astype(BF16)
    for c0 in range(0, tn, CHUNK):
        cs = slice(c0, c0 + CHUNK)
        pieces = jnp.concatenate([g_hi[cs], g_mid[cs], g_lo[cs]], axis=0)
        cf_ref[0, cs, :] = _dot(tri_f, pieces[:, :dk])
        cb_ref[0, cs, :] = _dot(tri_b, pieces[:, dk:])


def _gla_proj_call(x, mods, g, wq, wk, wv, wr, wg1, wg2, bg, tn):
    bsz, n, d = x.shape
    dk, dv = wq[0].shape[-1], wv[0].shape[-1]
    tile = lambda w: pl.BlockSpec((1, tn, w), lambda b, t: (b, t, 0))
    shp = lambda w: jax.ShapeDtypeStruct((bsz, n, w), F32)
    param_specs, param_arrays = _specs_and_arrays([g, wq, wk, wv, wr, wg1, wg2, bg])
    return pl.pallas_call(
        functools.partial(_gla_proj_kernel, q_scale=float((dk // GLA_HEADS) ** -0.5)),
        grid=(bsz, n // tn),
        in_specs=[tile(d), pl.BlockSpec((1, 6, d), lambda b, t: (b, 0, 0))] + param_specs,
        out_specs=[tile(dk), tile(dk), tile(dv), tile(dv), tile(dk), tile(dk)],
        out_shape=[shp(dk), shp(dk), shp(dv), jax.ShapeDtypeStruct((bsz, n, dv), BF16), shp(dk), shp(dk)],
        compiler_params=_params("parallel", "parallel"),
        name="gla_proj",
    )(x, mods, *param_arrays)


CHUNK_LEVELS = 6
assert 1 << CHUNK_LEVELS == CHUNK


def _level_reference(c_ref, base, level, reverse):
    half = 1 << level
    blk = 2 * half
    off = half if reverse else half - 1
    dk = c_ref.shape[-1]

    def row(r):
        return c_ref[0, pl.ds(base + (r + off), 1), :]

    if blk >= V7X_SUBLANES:
        return jnp.concatenate([jnp.broadcast_to(row(b0), (blk, dk)) for b0 in range(0, CHUNK, blk)], axis=0)
    sub = lax.broadcasted_iota(jnp.int32, (V7X_SUBLANES, dk), 0)
    groups = []
    for g0 in range(0, CHUNK, V7X_SUBLANES):
        r = jnp.broadcast_to(row(g0), (V7X_SUBLANES, dk))
        for b0 in range(blk, V7X_SUBLANES, blk):
            r = jnp.where(sub >= b0, jnp.broadcast_to(row(g0 + b0), (V7X_SUBLANES, dk)), r)
        groups.append(r)
    return jnp.concatenate(groups, axis=0)


def _gla_scan_kernel(q_ref, k_ref, v_ref, cf_ref, cb_ref, sf0_ref, sb0_ref, o_ref, sff_ref, sbf_ref,
                     stf_ref, stb_ref, incf_ref, incb_ref, *, nchunks):
    stf_ref[...] = sf0_ref[0, 0]
    stb_ref[...] = sb0_ref[0, 0]
    dk = q_ref.shape[-1]
    ti = lax.broadcasted_iota(jnp.int32, (CHUNK, CHUNK), 0)
    si = lax.broadcasted_iota(jnp.int32, (CHUNK, CHUNK), 1)
    x = ti ^ si
    lvl = sum([(x >= (1 << b)).astype(jnp.int32) for b in range(CHUNK_LEVELS)]) - 1
    row = lax.broadcasted_iota(jnp.int32, (CHUNK, dk), 0)
    below_split = [((row >> level) & 1) == 1 for level in range(CHUNK_LEVELS)]

    def chunk_scores(base):
        rows = pl.ds(base, CHUNK)
        q, k, cf, cb = q_ref[0, rows, :], k_ref[0, rows, :], cf_ref[0, rows, :], cb_ref[0, rows, :]
        kb = k.astype(BF16)
        scores = None
        for level in range(CHUNK_LEVELS):
            half = 1 << level
            if level == 0:
                xq = jnp.where(below_split[0], cf - pltpu.roll(cf, 1, 0), cb - pltpu.roll(cb, CHUNK - 1, 0))
                part = _dot_nt((q * jnp.exp2(xq)).astype(BF16), kb)
                scores = jnp.where(lvl == 0, part, 0.0)
                continue
            if half >= V7X_SUBLANES:
                xq, xk = [], []
                for b0 in range(0, CHUNK, 2 * half):
                    above, below = slice(b0, b0 + half), slice(b0 + half, b0 + 2 * half)
                    rf = cf_ref[0, pl.ds(base + (b0 + half - 1), 1), :]
                    rb = cb_ref[0, pl.ds(base + (b0 + half), 1), :]
                    xq += [cb[above] - rb, cf[below] - rf]
                    xk += [rf - cf[above], rb - cb[below]]
                xq, xk = jnp.concatenate(xq, axis=0), jnp.concatenate(xk, axis=0)
            else:
                df = cf - _level_reference(cf_ref, base, level, False)
                db = cb - _level_reference(cb_ref, base, level, True)
                xq = jnp.where(below_split[level], df, db)
                xk = -jnp.where(below_split[level], db, df)
            part = _dot_nt((q * jnp.exp2(xq)).astype(BF16), (k * jnp.exp2(xk)).astype(BF16))
            scores = jnp.where(lvl == level, part, scores)
        return scores.astype(BF16)

    def state_increment(base, c_ref, end_row):
        rows = pl.ds(base, CHUNK)
        c_end = c_ref[0, pl.ds(base + end_row, 1), :]
        kd = (k_ref[0, rows, :] * jnp.exp2(c_end - c_ref[0, rows, :])).astype(BF16)
        return _dot_tn(kd, v_ref[0, rows, :].astype(BF16))

    def carried(base, c_ref, end_row, st_ref, inc):
        rows = pl.ds(base, CHUNK)
        st = st_ref[...]
        o = _dot((q_ref[0, rows, :] * jnp.exp2(c_ref[0, rows, :])).astype(BF16), st.astype(BF16))
        c_end = c_ref[0, pl.ds(base + end_row, 1), :]
        decay = jnp.broadcast_to(jnp.exp2(c_end), (dk, dk)).T[:, 0:1]
        st_ref[...] = st * decay + inc
        return o

    def chunk_base(ci):
        return pl.multiple_of(ci * CHUNK, CHUNK)

    def produce(i, slot):
        fbase = chunk_base(jnp.minimum(i, nchunks - 1))
        bbase = chunk_base(jnp.maximum(nchunks - 1 - i, 0))
        incf_ref[slot] = state_increment(fbase, cf_ref, CHUNK - 1)
        incb_ref[slot] = state_increment(bbase, cb_ref, 0)
        return chunk_scores(fbase)

    def step(i, scores, first, slot):
        next_scores = produce(i + 1, 1 - slot)
        fbase, bbase = chunk_base(i), chunk_base(nchunks - 1 - i)
        frows, brows = pl.ds(fbase, CHUNK), pl.ds(bbase, CHUNK)
        q, k, v = q_ref[0, frows, :], k_ref[0, frows, :], v_ref[0, frows, :]
        diag = jnp.sum(q * k, axis=-1, keepdims=True)
        o_f = (_dot(scores, v.astype(BF16)) + (2.0 * diag) * v
               + carried(fbase, cf_ref, CHUNK - 1, stf_ref, incf_ref[slot]))
        o_b = carried(bbase, cb_ref, 0, stb_ref, incb_ref[slot])
        for rows, part in ((frows, o_f), (brows, o_b)):
            if not first:
                part = part + o_ref[0, rows, :].astype(F32)
            o_ref[0, rows, :] = part.astype(o_ref.dtype)
        return next_scores

    half = nchunks // 2
    per_trip = 8 if half % 8 == 0 else 2

    def trip(first):
        def body(j, s):
            for u in range(per_trip):
                s = step(per_trip * j + u, s, first, u % 2)
            return s
        return body

    scores = produce(0, 0)
    scores = lax.fori_loop(0, half // per_trip, trip(True), scores)
    lax.fori_loop(half // per_trip, nchunks // per_trip, trip(False), scores)
    sff_ref[0, 0] = stf_ref[...]
    sbf_ref[0, 0] = stb_ref[...]


def _gla_scan_call(q, k, v, cf, cb, s_f, s_b):
    bsz, n, dk = q.shape
    dv = v.shape[-1]
    hk, hv = dk // GLA_HEADS, dv // GLA_HEADS
    nchunks = n // CHUNK
    assert nchunks % 4 == 0
    tile = lambda w: pl.BlockSpec((1, n, w), lambda b, h: (b, 0, h))
    state = pl.BlockSpec((1, 1, hk, hv), lambda b, h: (b, h, 0, 0))
    state_shape = jax.ShapeDtypeStruct((bsz, GLA_HEADS, hk, hv), F32)
    return pl.pallas_call(
        functools.partial(_gla_scan_kernel, nchunks=nchunks),
        grid=(bsz, GLA_HEADS),
        in_specs=[tile(hk), tile(hk), tile(hv), tile(hk), tile(hk), state, state],
        out_specs=[tile(hv), state, state],
        out_shape=[jax.ShapeDtypeStruct((bsz, n, dv), BF16), state_shape, state_shape],
        scratch_shapes=[pltpu.VMEM((hk, hv), F32), pltpu.VMEM((hk, hv), F32),
                        pltpu.VMEM((2, hk, hv), F32), pltpu.VMEM((2, hk, hv), F32)],
        compiler_params=_params("parallel", "parallel"),
        name="gla_scan",
    )(q, k, v, cf, cb, s_f, s_b)


def _gla_out_kernel(o_in_ref, r_ref, ng_ref, wo_ref, x_ref, mod_ref, gpost_ref, o_ref, *, heads):
    o = o_in_ref[0].astype(F32)
    hv = o.shape[-1] // heads
    parts = []
    for h in range(heads):
        oh = o[:, h * hv:(h + 1) * hv]
        parts.append(oh * lax.rsqrt(jnp.mean(oh * oh, axis=-1, keepdims=True) + EPS))
    gated = jnp.concatenate(parts, axis=-1) * ng_ref[...] * r_ref[0].astype(F32)
    y = _dot(gated.astype(BF16), wo_ref[...])
    o_ref[0] = _residual(x_ref[0], y, gpost_ref[...], mod_ref, G1)


def _gla_out_call(o, r, ng, wo, x, mods, gpost, tn):
    bsz, n, d = x.shape
    dv = o.shape[-1]
    tile_v = pl.BlockSpec((1, tn, dv), lambda b, t: (b, t, 0))
    tile_d = pl.BlockSpec((1, tn, d), lambda b, t: (b, t, 0))
    return pl.pallas_call(
        functools.partial(_gla_out_kernel, heads=GLA_HEADS),
        grid=(bsz, n // tn),
        in_specs=[tile_v, tile_v, _layer_spec(ng), _layer_spec(wo), tile_d,
                  pl.BlockSpec((1, 6, d), lambda b, t: (b, 0, 0)), _layer_spec(gpost)],
        out_specs=tile_d,
        out_shape=jax.ShapeDtypeStruct((bsz, n, d), F32),
        compiler_params=_params("parallel", "parallel"),
        name="gla_out",
    )(o, r, ng[0], wo[0], x, mods, gpost[0])


MATMUL_TILE = 512
READOUT_TILE = 1024


def _tile(n, want):
    return min(n, want)


def _conformer(x, mods, g_pre, g_post, w):
    return _cf_call(x, mods, g_pre, g_post, w["w1"], w["b1"], w["dw"], w["dwb"], w["ln_g"], w["ln_b"], w["w2"],
                    w["b2"], _tile(x.shape[1], MATMUL_TILE))


def _gla_states(h, mods, g_pre, w, s_f, s_b):
    n = h.shape[1]
    q, k, v, r, cf, cb = _gla_proj_call(h, mods, g_pre, w["wq"], w["wk"], w["wv"], w["wr"], w["wg1"], w["wg2"],
                                        w["bg"], _tile(n, MATMUL_TILE))
    o, fin_f, fin_b = _gla_scan_call(q, k, v, cf, cb, s_f, s_b)
    return o, r, fin_f, fin_b


def _gla_readout(x, mods, g_post, w, o, r):
    return _gla_out_call(o, r, w["norm_g"], w["wo"], x, mods, g_post, _tile(x.shape[1], READOUT_TILE))


def _conv_ffn(x, mods, g_pre, g_post, w, on_grid):
    n = x.shape[1]
    if on_grid:
        return _ffn_call(x, mods, w["dw"], w["dwb"], w["wa"], w["wb"], w["wo"], g_pre, g_post, _tile(n, MATMUL_TILE),
                         GRID_W, True)
    return _ffn_call(x, mods, w["dw"], w["dwb"], w["wa"], w["wb"], w["wo"], g_pre, g_post, n, n, False)


def kernel(x, c, ctx, c_ctx, ada_w, ada_b, norm_pre_mix, norm_post_mix, norm_pre_ffn, norm_post_ffn, cf_w1, cf_b1, cf_dw, cf_dwb, cf_ln_g, cf_ln_b, cf_w2, cf_b2, gla_wq, gla_wk, gla_wv, gla_wr, gla_wg1, gla_wg2, gla_bg, gla_norm_g, gla_wo, ffn_wa, ffn_wb, ffn_dw, ffn_dwb, ffn_wo):
    bsz, n, d = x.shape
    depth = ada_w.shape[0]
    dk = gla_wq.shape[-1]
    dv = gla_wv.shape[-1]
    hk, hv = dk // GLA_HEADS, dv // GLA_HEADS
    rank = gla_wg1.shape[-1]
    rows = lambda v: v.reshape(v.shape[0], 1, -1)

    cond_rows = -(-(bsz + 1) // V7X_SUBLANES) * V7X_SUBLANES
    cond = jnp.zeros((cond_rows, d), F32).at[:bsz].set(c).at[bsz].set(c_ctx)
    mods = _ada_call(cond, ada_w, ada_b)

    pre_mix, post_mix, pre_ffn, post_ffn = rows(norm_pre_mix), rows(norm_post_mix), rows(norm_pre_ffn), rows(norm_post_ffn)
    cf = dict(w1=cf_w1.astype(BF16), b1=rows(cf_b1), dw=cf_dw, dwb=rows(cf_dwb), ln_g=rows(cf_ln_g),
              ln_b=rows(cf_ln_b), w2=cf_w2.astype(BF16), b2=rows(cf_b2))
    wg2 = jnp.zeros((gla_wg2.shape[0], 2 * rank, 2 * dk), F32)
    wg2 = wg2.at[:, :rank, :dk].set(gla_wg2[:, 0]).at[:, rank:, dk:].set(gla_wg2[:, 1])
    gla = dict(wq=gla_wq.astype(BF16), wk=gla_wk.astype(BF16), wv=gla_wv.astype(BF16), wr=gla_wr.astype(BF16),
               wg1=jnp.concatenate([gla_wg1[:, 0], gla_wg1[:, 1]], axis=-1).astype(BF16), wg2=wg2.astype(BF16),
               bg=gla_bg.reshape(gla_bg.shape[0], 1, 2 * dk), norm_g=rows(jnp.tile(gla_norm_g, (1, GLA_HEADS))),
               wo=gla_wo.astype(BF16))
    ffn = dict(wa=ffn_wa.astype(BF16), wb=ffn_wb.astype(BF16),
               dw=ffn_dw.reshape(depth, -1, ffn_dw.shape[-1]), dwb=rows(ffn_dwb), wo=ffn_wo.astype(BF16))
    layer = lambda params, idx: {name: (arr, idx) for name, arr in params.items()}

    for i in range(depth):
        last = i == depth - 1
        j = i // N_MIXERS
        mx = mods[i, :bsz].reshape(bsz, 6, d)
        mc = jnp.broadcast_to(mods[i, bsz].reshape(1, 6, d), (bsz, 6, d))
        g_pre, g_post = (pre_mix, i), (post_mix, i)
        if i % N_MIXERS == 0:
            w = layer(cf, j)
            x = _conformer(x, mx, g_pre, g_post, w)
            if not last:
                ctx_mixed = _conformer(ctx, mc, g_pre, g_post, w)
        else:
            w = layer(gla, j)
            zeros = jnp.zeros((bsz, GLA_HEADS, hk, hv), F32)
            oc, rc, s_f, s_b = _gla_states(ctx, mc, g_pre, w, zeros, zeros)
            if not last:
                ctx_mixed = _gla_readout(ctx, mc, g_post, w, oc, rc)
            ox, rx, _, _ = _gla_states(x, mx, g_pre, w, s_f, s_b)
            x = _gla_readout(x, mx, g_post, w, ox, rx)
        fw = layer(ffn, i)
        g_pre_f, g_post_f = (pre_ffn, i), (post_ffn, i)
        x = _conv_ffn(x, mx, g_pre_f, g_post_f, fw, True)
        if not last:
            ctx = _conv_ffn(ctx_mixed, mc, g_pre_f, g_post_f, fw, False)
    return x
```

```python
import functools

import jax
import jax.numpy as jnp
from jax import lax
from jax.experimental import pallas as pl
from jax.experimental.pallas import tpu as pltpu

EPS = 1e-6
N_MIXERS = 2
GRID_W = 64
GLA_HEADS = 4
GATE_TAU = 16.0
CHUNK = 64
LOG2_E = 1.4426950408889634

V7X_LANES = 128
V7X_SUBLANES = 8
V7X_VMEM_BYTES = 64 * 1024 * 1024
VMEM_LIMIT = V7X_VMEM_BYTES * 7 // 8

F32 = jnp.float32
BF16 = jnp.bfloat16

SH1, SC1, G1, SH2, SC2, G2 = range(6)


def _params(*sem):
    return pltpu.CompilerParams(dimension_semantics=sem, vmem_limit_bytes=VMEM_LIMIT)


def _layer_spec(param):
    arr, layer = param
    nd = arr.ndim - 1
    return pl.BlockSpec((None,) + arr.shape[1:], lambda *_: (layer,) + (0,) * nd, pipeline_mode=pl.Buffered(1))


def _specs_and_arrays(params):
    return [_layer_spec(p) for p in params], [p[0] for p in params]


def _rms(x, g):
    return x * lax.rsqrt(jnp.mean(x * x, axis=-1, keepdims=True) + EPS) * g


def _silu(x):
    return x * jax.nn.sigmoid(x)


def _norm_mod(x, g, mod_ref, shift_row, scale_row):
    h = _rms(x, g)
    return h * (1.0 + mod_ref[0, scale_row:scale_row + 1, :]) + mod_ref[0, shift_row:shift_row + 1, :]


def _residual(x, y, gpost, mod_ref, gate_row):
    return x + mod_ref[0, gate_row:gate_row + 1, :] * _rms(y, gpost)


def _dot(a, b):
    return jnp.dot(a, b, preferred_element_type=F32)


def _dot_nt(a, b):
    return lax.dot_general(a, b, (((1,), (1,)), ((), ())), preferred_element_type=F32)


def _dot_tn(a, b):
    return lax.dot_general(a, b, (((0,), (0,)), ((), ())), preferred_element_type=F32)


def _ada_kernel(cond_ref, w_ref, b_ref, o_ref):
    s = _silu(cond_ref[...])
    o_ref[0] = _dot(s.astype(BF16), w_ref[0].astype(BF16)) + b_ref[0]


def _ada_call(cond, ada_w, ada_b):
    depth, d, d6 = ada_w.shape
    rows = cond.shape[0]
    tn = d6 // 4
    return pl.pallas_call(
        _ada_kernel,
        grid=(depth, d6 // tn),
        in_specs=[
            pl.BlockSpec((rows, d), lambda i, j: (0, 0)),
            pl.BlockSpec((1, d, tn), lambda i, j: (i, 0, j)),
            pl.BlockSpec((1, 1, tn), lambda i, j: (i, 0, j)),
        ],
        out_specs=pl.BlockSpec((1, rows, tn), lambda i, j: (i, 0, j)),
        out_shape=jax.ShapeDtypeStruct((depth, rows, d6), F32),
        compiler_params=_params("parallel", "parallel"),
        name="ada",
    )(cond, ada_w, ada_b.reshape(depth, 1, d6))


CF_HALO = 16
CF_ROWS = 256
CF_LANES = 2 * V7X_LANES


def _cf_kernel(x_ref, xp_ref, xn_ref, mod_ref, gpre_ref, gpost_ref, w1_ref, b1_ref, dw_ref, dwb_ref, lng_ref,
               lnb_ref, w2_ref, b2_ref, o_ref, buf_ref, u_ref, *, width):
    t = pl.program_id(1)
    nt = pl.num_programs(1)
    tn, d = x_ref.shape[1:]
    pad = width // 2
    x = x_ref[0]
    gpre = gpre_ref[...]
    hb_ext = jnp.concatenate([_norm_mod(v, gpre, mod_ref, SH1, SC1).astype(BF16)
                              for v in (xp_ref[0], x, xn_ref[0])], axis=0)

    def stage_glu(cs):
        gate_cs = slice(d + cs.start, d + cs.stop)
        a = _dot(hb_ext, w1_ref[:, cs]) + b1_ref[:, cs]
        gt = _dot(hb_ext, w1_ref[:, gate_cs]) + b1_ref[:, gate_cs]
        glu = a * jax.nn.sigmoid(gt)
        buf_ref[0:CF_HALO, cs] = jnp.where(t > 0, glu[0:CF_HALO], 0.0)
        buf_ref[CF_HALO:CF_HALO + tn, cs] = glu[CF_HALO:CF_HALO + tn]
        buf_ref[CF_HALO + tn:, cs] = jnp.where(t < nt - 1, glu[CF_HALO + tn:], 0.0)

    lead = CF_HALO - pad
    groups = -(-(lead + width) // V7X_SUBLANES)
    strips = [slice(c0, c0 + CF_LANES) for c0 in range(0, d, CF_LANES)]
    stage_glu(strips[0])
    for si, cs in enumerate(strips):
        if si + 1 < len(strips):
            stage_glu(strips[si + 1])
        for l0 in range(cs.start, cs.stop, V7X_LANES):
            ls = slice(l0, l0 + V7X_LANES)
            for r0 in range(0, tn, CF_ROWS):
                acc = None
                for r in range(V7X_SUBLANES):
                    aligned = None
                    for a in range(groups):
                        k = V7X_SUBLANES * a + r - lead
                        if 0 <= k < width:
                            lo = r0 + V7X_SUBLANES * a
                            term = buf_ref[lo:lo + CF_ROWS + V7X_SUBLANES, ls] * dw_ref[k:k + 1, ls]
                            aligned = term if aligned is None else aligned + term
                    piece = aligned[r:r + CF_ROWS]
                    acc = piece if acc is None else acc + piece
                u_ref[r0:r0 + CF_ROWS, ls] = acc
    u = u_ref[...] + dwb_ref[...]
    mu = jnp.mean(u, axis=-1, keepdims=True)
    uc = u - mu
    var = jnp.mean(uc * uc, axis=-1, keepdims=True)
    v = _silu(uc * lax.rsqrt(var + EPS) * lng_ref[...] + lnb_ref[...])
    y = _dot(v.astype(BF16), w2_ref[...]) + b2_ref[...]
    o_ref[0] = _residual(x, y, gpost_ref[...], mod_ref, G1)


def _cf_call(x, mods, gpre, gpost, w1, b1, dw, dwb, lng, lnb, w2, b2, tn):
    bsz, n, d = x.shape
    width = dw[0].shape[1]
    assert width // 2 < CF_HALO and tn % CF_ROWS == 0 and d % CF_LANES == 0
    hb = tn // CF_HALO
    last = n // CF_HALO - 1
    tile = pl.BlockSpec((1, tn, d), lambda b, t: (b, t, 0))
    param_specs, param_arrays = _specs_and_arrays([gpre, gpost, w1, b1, dw, dwb, lng, lnb, w2, b2])
    return pl.pallas_call(
        functools.partial(_cf_kernel, width=width),
        grid=(bsz, n // tn),
        in_specs=[
            tile,
            pl.BlockSpec((1, CF_HALO, d), lambda b, t: (b, jnp.maximum(t * hb - 1, 0), 0)),
            pl.BlockSpec((1, CF_HALO, d), lambda b, t: (b, jnp.minimum((t + 1) * hb, last), 0)),
            pl.BlockSpec((1, 6, d), lambda b, t: (b, 0, 0)),
        ] + param_specs,
        out_specs=tile,
        out_shape=jax.ShapeDtypeStruct((bsz, n, d), F32),
        scratch_shapes=[pltpu.VMEM((tn + 2 * CF_HALO, d), F32), pltpu.VMEM((tn, d), F32)],
        compiler_params=_params("parallel", "parallel"),
        name="cf",
    )(x, x, x, mods, *param_arrays)


FFN_LANES = 8 * V7X_LANES


def _ffn_strips(f):
    return [slice(f0, min(f0 + FFN_LANES, f)) for f0 in range(0, f, FFN_LANES)]


def _ffn_kernel(*refs, gw, vertical):
    if vertical:
        x_ref, xp_ref, xn_ref = refs[:3]
        refs = refs[3:]
    else:
        x_ref = refs[0]
        refs = refs[1:]
    dw_ref, dwb_ref, wa_ref, wb_ref, wo_ref, mod_ref, gpre_ref, gpost_ref, o_ref, gated_ref = refs
    t = pl.program_id(1)
    nt = pl.num_programs(1)
    tn = x_ref.shape[1]
    f = wa_ref.shape[1]
    nrows = tn // gw
    x = x_ref[0]
    gpre = gpre_ref[...]
    hb = _norm_mod(x, gpre, mod_ref, SH2, SC2).astype(BF16)
    if vertical:
        before = jnp.where(t > 0, _norm_mod(xp_ref[0], gpre, mod_ref, SH2, SC2), 0.0).astype(BF16)
        after = jnp.where(t < nt - 1, _norm_mod(xn_ref[0], gpre, mod_ref, SH2, SC2), 0.0).astype(BF16)
        hb_ext = jnp.concatenate([before, hb, after], axis=0)
    else:
        hb_ext = hb
    y = None
    strips = _ffn_strips(f)

    a_next, gate_next = _dot(hb_ext, wa_ref[:, strips[0]]), _dot(hb, wb_ref[:, strips[0]])
    for si, fs in enumerate(strips):
        fl = fs.stop - fs.start
        a_ext, gate = a_next, gate_next
        sub = lax.broadcasted_iota(jnp.int32, (V7X_SUBLANES, fl), 0)
        w = [dw_ref[k:k + 1, fs] for k in range(dw_ref.shape[0])]
        bias = dwb_ref[:, fs]
        if si + 1 < len(strips):
            a_next = _dot(hb_ext, wa_ref[:, strips[si + 1]])
            gate_next = _dot(hb, wb_ref[:, strips[si + 1]])
        if si > 0:
            part = _dot(gated_ref[:, strips[si - 1]], wo_ref[strips[si - 1], :])
            y = part if y is None else y + part
        for i in range(nrows):
            rs = slice(i * gw, (i + 1) * gw)
            if vertical:
                up, mid, dn = [a_ext[(i + j) * gw:(i + j + 1) * gw] for j in range(3)]
                left, centre, right = [w[j] * up + w[3 + j] * mid + w[6 + j] * dn for j in range(3)]
            else:
                mid = a_ext[rs]
                left, centre, right = [w[3 + j] * mid for j in range(3)]
            from_left = pltpu.roll(left, 1, 0)
            from_left = jnp.concatenate([jnp.where(sub > 0, from_left[:V7X_SUBLANES], 0.0),
                                         from_left[V7X_SUBLANES:]], axis=0)
            from_right = pltpu.roll(right, gw - 1, 0)
            from_right = jnp.concatenate([from_right[:gw - V7X_SUBLANES],
                                          jnp.where(sub < V7X_SUBLANES - 1, from_right[gw - V7X_SUBLANES:], 0.0)],
                                         axis=0)
            conv = centre + bias + from_left + from_right
            gated_ref[rs, fs] = (_silu(conv) * gate[rs]).astype(BF16)
    part = _dot(gated_ref[:, strips[-1]], wo_ref[strips[-1], :])
    y = part if y is None else y + part
    o_ref[0] = _residual(x, y, gpost_ref[...], mod_ref, G2)


def _ffn_call(x, mods, dw, dwb, wa, wb, wo, gpre, gpost, tn, gw, vertical):
    bsz, n, d = x.shape
    f = wa[0].shape[-1]
    assert tn % gw == 0
    hb = tn // gw
    last = n // gw - 1
    tile_d = pl.BlockSpec((1, tn, d), lambda b, t: (b, t, 0))
    in_specs = [tile_d]
    args = [x]
    if vertical:
        in_specs += [pl.BlockSpec((1, gw, d), lambda b, t: (b, jnp.maximum(t * hb - 1, 0), 0)),
                     pl.BlockSpec((1, gw, d), lambda b, t: (b, jnp.minimum((t + 1) * hb, last), 0))]
        args += [x, x]
    param_specs, param_arrays = _specs_and_arrays([dw, dwb, wa, wb, wo])
    norm_specs, norm_arrays = _specs_and_arrays([gpre, gpost])
    in_specs += param_specs + [pl.BlockSpec((1, 6, d), lambda b, t: (b, 0, 0))] + norm_specs
    args += param_arrays + [mods] + norm_arrays
    return pl.pallas_call(
        functools.partial(_ffn_kernel, gw=gw, vertical=vertical),
        grid=(bsz, n // tn),
        in_specs=in_specs,
        out_specs=tile_d,
        out_shape=jax.ShapeDtypeStruct((bsz, n, d), F32),
        scratch_shapes=[pltpu.VMEM((tn, f), BF16)],
        compiler_params=_params("parallel", "parallel"),
        name="ffn_grid" if vertical else "ffn_seq",
    )(*args)


def _log_sigmoid(z):
    return jnp.minimum(z, 0.0) - jnp.log(1.0 + jnp.exp(-jnp.abs(z)))


def _gla_proj_kernel(x_ref, mod_ref, g_ref, wq_ref, wk_ref, wv_ref, wr_ref, wg1_ref, wg2_ref, bg_ref,
                     q_ref, k_ref, v_ref, r_ref, cf_ref, cb_ref, *, q_scale):
    tn = x_ref.shape[1]
    dk = q_ref.shape[-1]
    hb = _norm_mod(x_ref[0], g_ref[...], mod_ref, SH1, SC1).astype(BF16)
    low = _dot(hb, wg1_ref[...]).astype(BF16)
    z = _dot(low, wg2_ref[...]) + bg_ref[...]
    q_ref[0] = _dot(hb, wq_ref[...]) * q_scale
    k_ref[0] = _dot(hb, wk_ref[...])
    g = _log_sigmoid(z) * (LOG2_E / GATE_TAU)
    v_ref[0] = _dot(hb, wv_ref[...])
    r_ref[0] = _silu(_dot(hb, wr_ref[...])).astype(r_ref.dtype)
    g_hi = g.astype(BF16)
    rest = g - g_hi.astype(F32)
    g_mid = rest.astype(BF16)
    g_lo = (rest - g_mid.astype(F32)).astype(BF16)
    row = lax.broadcasted_iota(jnp.int32, (CHUNK, 3 * CHUNK), 0)
    colm = lax.broadcasted_iota(jnp.int32, (CHUNK, 3 * CHUNK), 1) & (CHUNK - 1)
    tri_f = jnp.where(colm <= row, 1.0, 0.0).astype(BF16)
    tri_b = jnp.where(colm >= row, 1.0, 0.0).astype(BF16)
    for c0 in range(0, tn, CHUNK):
        cs = slice(c0, c0 + CHUNK)
        pieces = jnp.concatenate([g_hi[cs], g_mid[cs], g_lo[cs]], axis=0)
        cf_ref[0, cs, :] = _dot(tri_f, pieces[:, :dk])
        cb_ref[0, cs, :] = _dot(tri_b, pieces[:, dk:])


def _gla_proj_call(x, mods, g, wq, wk, wv, wr, wg1, wg2, bg, tn):
    bsz, n, d = x.shape
    dk, dv = wq[0].shape[-1], wv[0].shape[-1]
    tile = lambda w: pl.BlockSpec((1, tn, w), lambda b, t: (b, t, 0))
    shp = lambda w: jax.ShapeDtypeStruct((bsz, n, w), F32)
    param_specs, param_arrays = _specs_and_arrays([g, wq, wk, wv, wr, wg1, wg2, bg])
    return pl.pallas_call(
        functools.partial(_gla_proj_kernel, q_scale=float((dk // GLA_HEADS) ** -0.5)),
        grid=(bsz, n // tn),
        in_specs=[tile(d), pl.BlockSpec((1, 6, d), lambda b, t: (b, 0, 0))] + param_specs,
        out_specs=[tile(dk), tile(dk), tile(dv), tile(dv), tile(dk), tile(dk)],
        out_shape=[shp(dk), shp(dk), shp(dv), jax.ShapeDtypeStruct((bsz, n, dv), BF16), shp(dk), shp(dk)],
        compiler_params=_params("parallel", "parallel"),
        name="gla_proj",
    )(x, mods, *param_arrays)


CHUNK_LEVELS = 6
assert 1 << CHUNK_LEVELS == CHUNK


def _level_reference(c_ref, base, level, reverse):
    half = 1 << level
    blk = 2 * half
    off = half if reverse else half - 1
    dk = c_ref.shape[-1]

    def row(r):
        return c_ref[0, pl.ds(base + (r + off), 1), :]

    if blk >= V7X_SUBLANES:
        return jnp.concatenate([jnp.broadcast_to(row(b0), (blk, dk)) for b0 in range(0, CHUNK, blk)], axis=0)
    sub = lax.broadcasted_iota(jnp.int32, (V7X_SUBLANES, dk), 0)
    groups = []
    for g0 in range(0, CHUNK, V7X_SUBLANES):
        r = jnp.broadcast_to(row(g0), (V7X_SUBLANES, dk))
        for b0 in range(blk, V7X_SUBLANES, blk):
            r = jnp.where(sub >= b0, jnp.broadcast_to(row(g0 + b0), (V7X_SUBLANES, dk)), r)
        groups.append(r)
    return jnp.concatenate(groups, axis=0)


def _gla_scan_kernel(q_ref, k_ref, v_ref, cf_ref, cb_ref, sf0_ref, sb0_ref, o_ref, sff_ref, sbf_ref,
                     stf_ref, stb_ref, incf_ref, incb_ref, *, nchunks):
    stf_ref[...] = sf0_ref[0, 0]
    stb_ref[...] = sb0_ref[0, 0]
    dk = q_ref.shape[-1]
    ti = lax.broadcasted_iota(jnp.int32, (CHUNK, CHUNK), 0)
    si = lax.broadcasted_iota(jnp.int32, (CHUNK, CHUNK), 1)
    x = ti ^ si
    lvl = sum([(x >= (1 << b)).astype(jnp.int32) for b in range(CHUNK_LEVELS)]) - 1
    row = lax.broadcasted_iota(jnp.int32, (CHUNK, dk), 0)
    below_split = [((row >> level) & 1) == 1 for level in range(CHUNK_LEVELS)]

    def chunk_scores(base):
        rows = pl.ds(base, CHUNK)
        q, k, cf, cb = q_ref[0, rows, :], k_ref[0, rows, :], cf_ref[0, rows, :], cb_ref[0, rows, :]
        kb = k.astype(BF16)
        scores = None
        for level in range(CHUNK_LEVELS):
            half = 1 << level
            if level == 0:
                xq = jnp.where(below_split[0], cf - pltpu.roll(cf, 1, 0), cb - pltpu.roll(cb, CHUNK - 1, 0))
                part = _dot_nt((q * jnp.exp2(xq)).astype(BF16), kb)
                scores = jnp.where(lvl == 0, part, 0.0)
                continue
            if half >= V7X_SUBLANES:
                xq, xk = [], []
                for b0 in range(0, CHUNK, 2 * half):
                    above, below = slice(b0, b0 + half), slice(b0 + half, b0 + 2 * half)
                    rf = cf_ref[0, pl.ds(base + (b0 + half - 1), 1), :]
                    rb = cb_ref[0, pl.ds(base + (b0 + half), 1), :]
                    xq += [cb[above] - rb, cf[below] - rf]
                    xk += [rf - cf[above], rb - cb[below]]
                xq, xk = jnp.concatenate(xq, axis=0), jnp.concatenate(xk, axis=0)
            else:
                df = cf - _level_reference(cf_ref, base, level, False)
                db = cb - _level_reference(cb_ref, base, level, True)
                xq = jnp.where(below_split[level], df, db)
                xk = -jnp.where(below_split[level], db, df)
            part = _dot_nt((q * jnp.exp2(xq)).astype(BF16), (k * jnp.exp2(xk)).astype(BF16))
            scores = jnp.where(lvl == level, part, scores)
        return scores.astype(BF16)

    def state_increment(base, c_ref, end_row):
        rows = pl.ds(base, CHUNK)
        c_end = c_ref[0, pl.ds(base + end_row, 1), :]
        kd = (k_ref[0, rows, :] * jnp.exp2(c_end - c_ref[0, rows, :])).astype(BF16)
        return _dot_tn(kd, v_ref[0, rows, :].astype(BF16))

    def carried(base, c_ref, end_row, st_ref, inc):
        rows = pl.ds(base, CHUNK)
        st = st_ref[...]
        o = _dot((q_ref[0, rows, :] * jnp.exp2(c_ref[0, rows, :])).astype(BF16), st.astype(BF16))
        c_end = c_ref[0, pl.ds(base + end_row, 1), :]
        decay = jnp.broadcast_to(jnp.exp2(c_end), (dk, dk)).T[:, 0:1]
        st_ref[...] = st * decay + inc
        return o

    def chunk_base(ci):
        return pl.multiple_of(ci * CHUNK, CHUNK)

    def produce(i, slot):
        fbase = chunk_base(jnp.minimum(i, nchunks - 1))
        bbase = chunk_base(jnp.maximum(nchunks - 1 - i, 0))
        incf_ref[slot] = state_increment(fbase, cf_ref, CHUNK - 1)
        incb_ref[slot] = state_increment(bbase, cb_ref, 0)
        return chunk_scores(fbase)

    def step(i, scores, first, slot):
        next_scores = produce(i + 1, 1 - slot)
        fbase, bbase = chunk_base(i), chunk_base(nchunks - 1 - i)
        frows, brows = pl.ds(fbase, CHUNK), pl.ds(bbase, CHUNK)
        q, k, v = q_ref[0, frows, :], k_ref[0, frows, :], v_ref[0, frows, :]
        diag = jnp.sum(q * k, axis=-1, keepdims=True)
        o_f = (_dot(scores, v.astype(BF16)) + (2.0 * diag) * v
               + carried(fbase, cf_ref, CHUNK - 1, stf_ref, incf_ref[slot]))
        o_b = carried(bbase, cb_ref, 0, stb_ref, incb_ref[slot])
        for rows, part in ((frows, o_f), (brows, o_b)):
            if not first:
                part = part + o_ref[0, rows, :].astype(F32)
            o_ref[0, rows, :] = part.astype(o_ref.dtype)
        return next_scores

    half = nchunks // 2
    per_trip = 8 if half % 8 == 0 else 2

    def trip(first):
        def body(j, s):
            for u in range(per_trip):
                s = step(per_trip * j + u, s, first, u % 2)
            return s
        return body

    scores = produce(0, 0)
    scores = lax.fori_loop(0, half // per_trip, trip(True), scores)
    lax.fori_loop(half // per_trip, nchunks // per_trip, trip(False), scores)
    sff_ref[0, 0] = stf_ref[...]
    sbf_ref[0, 0] = stb_ref[...]


def _gla_scan_call(q, k, v, cf, cb, s_f, s_b):
    bsz, n, dk = q.shape
    dv = v.shape[-1]
    hk, hv = dk // GLA_HEADS, dv // GLA_HEADS
    nchunks = n // CHUNK
    assert nchunks % 4 == 0
    tile = lambda w: pl.BlockSpec((1, n, w), lambda b, h: (b, 0, h))
    state = pl.BlockSpec((1, 1, hk, hv), lambda b, h: (b, h, 0, 0))
    state_shape = jax.ShapeDtypeStruct((bsz, GLA_HEADS, hk, hv), F32)
    return pl.pallas_call(
        functools.partial(_gla_scan_kernel, nchunks=nchunks),
        grid=(bsz, GLA_HEADS),
        in_specs=[tile(hk), tile(hk), tile(hv), tile(hk), tile(hk), state, state],
        out_specs=[tile(hv), state, state],
        out_shape=[jax.ShapeDtypeStruct((bsz, n, dv), BF16), state_shape, state_shape],
        scratch_shapes=[pltpu.VMEM((hk, hv), F32), pltpu.VMEM((hk, hv), F32),
                        pltpu.VMEM((2, hk, hv), F32), pltpu.VMEM((2, hk, hv), F32)],
        compiler_params=_params("parallel", "parallel"),
        name="gla_scan",
    )(q, k, v, cf, cb, s_f, s_b)


def _gla_out_kernel(o_in_ref, r_ref, ng_ref, wo_ref, x_ref, mod_ref, gpost_ref, o_ref, *, heads):
    o = o_in_ref[0].astype(F32)
    hv = o.shape[-1] // heads
    parts = []
    for h in range(heads):
        oh = o[:, h * hv:(h + 1) * hv]
        parts.append(oh * lax.rsqrt(jnp.mean(oh * oh, axis=-1, keepdims=True) + EPS))
    gated = jnp.concatenate(parts, axis=-1) * ng_ref[...] * r_ref[0].astype(F32)
    y = _dot(gated.astype(BF16), wo_ref[...])
    o_ref[0] = _residual(x_ref[0], y, gpost_ref[...], mod_ref, G1)


def _gla_out_call(o, r, ng, wo, x, mods, gpost, tn):
    bsz, n, d = x.shape
    dv = o.shape[-1]
    tile_v = pl.BlockSpec((1, tn, dv), lambda b, t: (b, t, 0))
    tile_d = pl.BlockSpec((1, tn, d), lambda b, t: (b, t, 0))
    return pl.pallas_call(
        functools.partial(_gla_out_kernel, heads=GLA_HEADS),
        grid=(bsz, n // tn),
        in_specs=[tile_v, tile_v, _layer_spec(ng), _layer_spec(wo), tile_d,
                  pl.BlockSpec((1, 6, d), lambda b, t: (b, 0, 0)), _layer_spec(gpost)],
        out_specs=tile_d,
        out_shape=jax.ShapeDtypeStruct((bsz, n, d), F32),
        compiler_params=_params("parallel", "parallel"),
        name="gla_out",
    )(o, r, ng[0], wo[0], x, mods, gpost[0])


MATMUL_TILE = 512
READOUT_TILE = 1024


def _tile(n, want):
    return min(n, want)


def _conformer(x, mods, g_pre, g_post, w):
    return _cf_call(x, mods, g_pre, g_post, w["w1"], w["b1"], w["dw"], w["dwb"], w["ln_g"], w["ln_b"], w["w2"],
                    w["b2"], _tile(x.shape[1], MATMUL_TILE))


def _gla_states(h, mods, g_pre, w, s_f, s_b):
    n = h.shape[1]
    q, k, v, r, cf, cb = _gla_proj_call(h, mods, g_pre, w["wq"], w["wk"], w["wv"], w["wr"], w["wg1"], w["wg2"],
                                        w["bg"], _tile(n, MATMUL_TILE))
    o, fin_f, fin_b = _gla_scan_call(q, k, v, cf, cb, s_f, s_b)
    return o, r, fin_f, fin_b


def _gla_readout(x, mods, g_post, w, o, r):
    return _gla_out_call(o, r, w["norm_g"], w["wo"], x, mods, g_post, _tile(x.shape[1], READOUT_TILE))


def _conv_ffn(x, mods, g_pre, g_post, w, on_grid):
    n = x.shape[1]
    if on_grid:
        return _ffn_call(x, mods, w["dw"], w["dwb"], w["wa"], w["wb"], w["wo"], g_pre, g_post, _tile(n, MATMUL_TILE),
                         GRID_W, True)
    return _ffn_call(x, mods, w["dw"], w["dwb"], w["wa"], w["wb"], w["wo"], g_pre, g_post, n, n, False)


def kernel(x, c, ctx, c_ctx, ada_w, ada_b, norm_pre_mix, norm_post_mix, norm_pre_ffn, norm_post_ffn, cf_w1, cf_b1, cf_dw, cf_dwb, cf_ln_g, cf_ln_b, cf_w2, cf_b2, gla_wq, gla_wk, gla_wv, gla_wr, gla_wg1, gla_wg2, gla_bg, gla_norm_g, gla_wo, ffn_wa, ffn_wb, ffn_dw, ffn_dwb, ffn_wo):
    bsz, n, d = x.shape
    depth = ada_w.shape[0]
    dk = gla_wq.shape[-1]
    dv = gla_wv.shape[-1]
    hk, hv = dk // GLA_HEADS, dv // GLA_HEADS
    rank = gla_wg1.shape[-1]
    rows = lambda v: v.reshape(v.shape[0], 1, -1)

    cond_rows = -(-(bsz + 1) // V7X_SUBLANES) * V7X_SUBLANES
    cond = jnp.zeros((cond_rows, d), F32).at[:bsz].set(c).at[bsz].set(c_ctx)
    mods = _ada_call(cond, ada_w, ada_b)

    pre_mix, post_mix, pre_ffn, post_ffn = rows(norm_pre_mix), rows(norm_post_mix), rows(norm_pre_ffn), rows(norm_post_ffn)
    cf = dict(w1=cf_w1.astype(BF16), b1=rows(cf_b1), dw=cf_dw, dwb=rows(cf_dwb), ln_g=rows(cf_ln_g),
              ln_b=rows(cf_ln_b), w2=cf_w2.astype(BF16), b2=rows(cf_b2))
    wg2 = jnp.zeros((gla_wg2.shape[0], 2 * rank, 2 * dk), F32)
    wg2 = wg2.at[:, :rank, :dk].set(gla_wg2[:, 0]).at[:, rank:, dk:].set(gla_wg2[:, 1])
    gla = dict(wq=gla_wq.astype(BF16), wk=gla_wk.astype(BF16), wv=gla_wv.astype(BF16), wr=gla_wr.astype(BF16),
               wg1=jnp.concatenate([gla_wg1[:, 0], gla_wg1[:, 1]], axis=-1).astype(BF16), wg2=wg2.astype(BF16),
               bg=gla_bg.reshape(gla_bg.shape[0], 1, 2 * dk), norm_g=rows(jnp.tile(gla_norm_g, (1, GLA_HEADS))),
               wo=gla_wo.astype(BF16))
    ffn = dict(wa=ffn_wa.astype(BF16), wb=ffn_wb.astype(BF16),
               dw=ffn_dw.reshape(depth, -1, ffn_dw.shape[-1]), dwb=rows(ffn_dwb), wo=ffn_wo.astype(BF16))
    layer = lambda params, idx: {name: (arr, idx) for name, arr in params.items()}

    for i in range(depth):
        last = i == depth - 1
        j = i // N_MIXERS
        mx = mods[i, :bsz].reshape(bsz, 6, d)
        mc = jnp.broadcast_to(mods[i, bsz].reshape(1, 6, d), (bsz, 6, d))
        g_pre, g_post = (pre_mix, i), (post_mix, i)
        if i % N_MIXERS == 0:
            w = layer(cf, j)
            x = _conformer(x, mx, g_pre, g_post, w)
            if not last:
                ctx_mixed = _conformer(ctx, mc, g_pre, g_post, w)
        else:
            w = layer(gla, j)
            zeros = jnp.zeros((bsz, GLA_HEADS, hk, hv), F32)
            oc, rc, s_f, s_b = _gla_states(ctx, mc, g_pre, w, zeros, zeros)
            if not last:
                ctx_mixed = _gla_readout(ctx, mc, g_post, w, oc, rc)
            ox, rx, _, _ = _gla_states(x, mx, g_pre, w, s_f, s_b)
            x = _gla_readout(x, mx, g_post, w, ox, rx)
        fw = layer(ffn, i)
        g_pre_f, g_post_f = (pre_ffn, i), (post_ffn, i)
        x = _conv_ffn(x, mx, g_pre_f, g_post_f, fw, True)
        if not last:
            ctx = _conv_ffn(ctx_mixed, mc, g_pre_f, g_post_f, fw, False)
    return x
```
